```python
import jax, jax.numpy as jnp
from jax import lax
import numpy as np

D_MODEL = 1024
BATCH = 4
SEQ = 4096
DEPTH = 2

HEAD_DIM = 64
N_FOX_HEADS = 8
N_NSA_HEADS = 8
N_NSA_KV = 2
NSA_GROUP = N_NSA_HEADS // N_NSA_KV
CMP_LEN = 32
CMP_STRIDE = 16
SLC_LEN = 64
SLC_TOPK = 16
WINDOW = 512
Q_BLOCK = 128
FORCE_BONUS = 1.0e4
NEG = -1.0e30
ATTN_WIDTH = (N_FOX_HEADS + N_NSA_HEADS) * HEAD_DIM
FOX_QKV = 3 * N_FOX_HEADS * HEAD_DIM
FOX_F = N_FOX_HEADS
NSA_Q = N_NSA_HEADS * HEAD_DIM
NSA_KV = 6 * N_NSA_KV * HEAD_DIM
NSA_GATES = 3 * N_NSA_HEADS
EVEN_IN = FOX_QKV + FOX_F + NSA_Q + NSA_KV + NSA_GATES

D_RNN = 1280
RNN_BLOCKS = 10
RNN_BLOCK_W = D_RNN // RNN_BLOCKS
CONV_W = 4
LRU_C = 8.0

PEER_HEADS = 8
PEER_NKEYS = 128
PEER_NEXP = PEER_NKEYS * PEER_NKEYS
PEER_DKEY = 256
PEER_HALF = PEER_DKEY // 2
PEER_TOPK = 16
TOKEN_BLOCK = 128

RMS_EPS = 1e-6

kernel_name = 'fox_nsa_rglru_peer_hybrid'


def rmsnorm(x, g):
    xf = x.astype(jnp.float32)
    y = xf * lax.rsqrt(jnp.mean(xf * xf, axis=-1, keepdims=True) + RMS_EPS)
    return (y * g.astype(jnp.float32)).astype(x.dtype)


def alibi_slopes(n):
    return jnp.asarray(2.0 ** (-8.0 * np.arange(1, n + 1) / n), jnp.float32)


def masked_softmax(s, mask):
    return jax.nn.softmax(jnp.where(mask, s, NEG), axis=-1)


def fox_attention(q, k, v, logf):
    B, S, H, dh = q.shape
    cT = jnp.cumsum(logf, axis=1).transpose(0, 2, 1)
    qs = q * (dh ** -0.5)
    kpos = jnp.arange(S)

    def block(i):
        t0 = i * Q_BLOCK
        tpos = t0 + jnp.arange(Q_BLOCK)
        qb = lax.dynamic_slice_in_dim(qs, t0, Q_BLOCK, axis=1)
        cb = lax.dynamic_slice_in_dim(cT, t0, Q_BLOCK, axis=2)
        s = jnp.einsum('bthd,bshd->bhts', qb, k) + cb[..., None] - cT[:, :, None, :]
        p = masked_softmax(s, kpos[None, :] <= tpos[:, None])
        return jnp.einsum('bhts,bshd->bthd', p, v)

    out = lax.map(block, jnp.arange(S // Q_BLOCK))
    return out.swapaxes(0, 1).reshape(B, S, H * dh)


def nsa_attention(q, k_cmp, v_cmp, k_slc, v_slc, k_win, v_win, gates, cmp_pos, w_cmp_k, w_cmp_v):
    B, S, H, dh = q.shape
    G, R = N_NSA_KV, NSA_GROUP
    f32 = jnp.float32
    n_cmp = (S - CMP_LEN) // CMP_STRIDE + 1
    n_sel = S // SLC_LEN
    k_sel = min(SLC_TOPK, n_sel)
    slopes = alibi_slopes(H).reshape(G, R)
    cstart = CMP_STRIDE * np.arange(n_cmp)
    cidx = cstart[:, None] + np.arange(CMP_LEN)[None, :]
    kc = jnp.einsum('bnlgd,lde->bnge', k_cmp[:, cidx] + cmp_pos[:, None, :], w_cmp_k)
    vc = jnp.einsum('bnlgd,lde->bnge', v_cmp[:, cidx] + cmp_pos[:, None, :], w_cmp_v)
    cend = jnp.asarray(cstart + CMP_LEN - 1, jnp.int32)
    sstart = SLC_LEN * np.arange(n_sel)
    overlap = jnp.asarray((cstart[:, None] < sstart[None, :] + SLC_LEN)
                          & (cstart[:, None] + CMP_LEN > sstart[None, :]), f32)
    ks = k_slc.reshape(B, n_sel, SLC_LEN, G, dh).transpose(0, 3, 1, 2, 4)
    vs = v_slc.reshape(B, n_sel, SLC_LEN, G, dh).transpose(0, 3, 1, 2, 4)
    kw_pad = jnp.pad(k_win, ((0, 0), (WINDOW, 0), (0, 0), (0, 0)))
    vw_pad = jnp.pad(v_win, ((0, 0), (WINDOW, 0), (0, 0), (0, 0)))
    qg = q.reshape(B, S, G, R, dh) * (dh ** -0.5)
    gg = gates.reshape(B, S, G, R, 3)
    jpos = jnp.arange(n_sel)
    gather_blocks = jax.vmap(jax.vmap(lambda tab, ix: tab[ix]))
    M = k_sel * SLC_LEN

    def block(i):
        t0 = i * Q_BLOCK
        tpos = t0 + jnp.arange(Q_BLOCK)
        qb = lax.dynamic_slice_in_dim(qg, t0, Q_BLOCK, axis=1)
        gb = lax.dynamic_slice_in_dim(gg, t0, Q_BLOCK, axis=1)
        cmask = cend[None, :] <= tpos[:, None]
        s_c = (jnp.einsum('btgrd,bngd->bgrtn', qb, kc)
               - slopes[:, :, None, None] * (tpos[:, None] - cend[None, :]).astype(f32))
        p_c = masked_softmax(s_c, cmask) * jnp.any(cmask, axis=-1)[:, None].astype(f32)
        o_c = jnp.einsum('bgrtn,bngd->btgrd', p_c, vc)
        imp = jnp.einsum('bgrtn,nj->bgtj', p_c, overlap)
        cur = tpos // SLC_LEN
        forced = ((jpos[None, :] == 0) | (jpos[None, :] == cur[:, None])
                  | (jpos[None, :] == cur[:, None] - 1))
        imp = jnp.where(jpos[None, :] <= cur[:, None], imp + FORCE_BONUS * forced.astype(f32), -jnp.inf)
        _, sel = lax.top_k(imp, k_sel)
        gk = gather_blocks(ks, sel).reshape(B, G, Q_BLOCK, M, dh)
        gv = gather_blocks(vs, sel).reshape(B, G, Q_BLOCK, M, dh)
        spos = (sel[..., None] * SLC_LEN + jnp.arange(SLC_LEN)).reshape(B, G, Q_BLOCK, M)
        sdist = (tpos[:, None] - spos).astype(f32)
        s_s = (jnp.einsum('btgrd,bgtmd->bgrtm', qb, gk)
               - slopes[None, :, :, None, None] * sdist[:, :, None])
        p_s = masked_softmax(s_s, (sdist >= 0)[:, :, None])
        o_s = jnp.einsum('bgrtm,bgtmd->btgrd', p_s, gv)
        kw = lax.dynamic_slice_in_dim(kw_pad, t0, Q_BLOCK + WINDOW, axis=1)
        vw = lax.dynamic_slice_in_dim(vw_pad, t0, Q_BLOCK + WINDOW, axis=1)
        wpos = t0 - WINDOW + jnp.arange(Q_BLOCK + WINDOW)
        wdist = tpos[:, None] - wpos[None, :]
        wmask = (wdist >= 0) & (wdist < WINDOW) & (wpos[None, :] >= 0)
        s_w = (jnp.einsum('btgrd,bsgd->bgrts', qb, kw)
               - slopes[:, :, None, None] * wdist.astype(f32))
        o_w = jnp.einsum('bgrts,bsgd->btgrd', masked_softmax(s_w, wmask), vw)
        return gb[..., 0:1] * o_c + gb[..., 1:2] * o_s + gb[..., 2:3] * o_w

    out = lax.map(block, jnp.arange(S // Q_BLOCK))
    return out.swapaxes(0, 1).reshape(B, S, H * dh)


def even_mixer(h, w_in, b_f, cmp_pos, w_cmp_k, w_cmp_v, w_out):
    B, S, _ = h.shape
    z = jnp.einsum('bsd,de->bse', h, w_in).astype(jnp.float32)
    splits = [int(v) for v in np.cumsum([FOX_QKV, FOX_F, NSA_Q, NSA_KV])]
    fox_qkv, fox_f, nsa_q, nsa_kv, nsa_g = jnp.split(z, splits, axis=-1)
    fqkv = fox_qkv.reshape(B, S, 3, N_FOX_HEADS, HEAD_DIM)
    logf = jax.nn.log_sigmoid(fox_f + b_f.astype(jnp.float32))
    o_fox = fox_attention(fqkv[:, :, 0], fqkv[:, :, 1], fqkv[:, :, 2], logf)
    kv = nsa_kv.reshape(B, S, 6, N_NSA_KV, HEAD_DIM)
    gates = jax.nn.sigmoid(nsa_g).reshape(B, S, N_NSA_HEADS, 3)
    o_nsa = nsa_attention(nsa_q.reshape(B, S, N_NSA_HEADS, HEAD_DIM),
                          kv[:, :, 0], kv[:, :, 1], kv[:, :, 2], kv[:, :, 3], kv[:, :, 4], kv[:, :, 5],
                          gates, cmp_pos.astype(jnp.float32), w_cmp_k.astype(jnp.float32),
                          w_cmp_v.astype(jnp.float32))
    o = jnp.concatenate([o_fox, o_nsa], axis=-1)
    return jnp.einsum('bse,ed->bsd', o, w_out.astype(jnp.float32)).astype(h.dtype)


def rg_lru(x, w_ra, b_ra, w_ri, b_ri, lam):
    B, S, C = x.shape
    f32 = jnp.float32
    xf = x.astype(f32)
    xb = xf.reshape(B, S, RNN_BLOCKS, RNN_BLOCK_W)
    r = jax.nn.sigmoid(jnp.einsum('bsnc,nce->bsne', xb, w_ra.astype(f32)).reshape(B, S, C) + b_ra.astype(f32))
    i = jax.nn.sigmoid(jnp.einsum('bsnc,nce->bsne', xb, w_ri.astype(f32)).reshape(B, S, C) + b_ri.astype(f32))
    log_a = -LRU_C * r * jax.nn.softplus(-lam.astype(f32))
    a = jnp.exp(log_a)
    u = jnp.sqrt(-jnp.expm1(2.0 * log_a)) * (i * xf)

    def combine(left, right):
        a1, b1 = left
        a2, b2 = right
        return a1 * a2, a2 * b1 + b2

    _, hs = lax.associative_scan(combine, (a, u), axis=1)
    return hs


def odd_mixer(h, w_in, conv_w, conv_b, w_ra, b_ra, w_ri, b_ri, lam, w_out):
    z = jnp.einsum('bsd,de->bse', h, w_in)
    gate, xr = jnp.split(z, 2, axis=-1)
    xr = lax.conv_general_dilated(xr, conv_w[:, None, :].astype(xr.dtype), window_strides=(1,),
                                  padding=((CONV_W - 1, 0),),
                                  dimension_numbers=('NWC', 'WIO', 'NWC'),
                                  feature_group_count=D_RNN) + conv_b.astype(xr.dtype)
    y = rg_lru(xr, w_ra, b_ra, w_ri, b_ri, lam)
    y = jax.nn.gelu(gate.astype(jnp.float32), approximate=False) * y
    return jnp.einsum('bse,ed->bsd', y, w_out.astype(jnp.float32)).astype(h.dtype)


def peer_ffn(h, w_q, sub_keys, u_tab, v_tab):
    B, S, D = h.shape
    K = PEER_TOPK
    q = jnp.einsum('bsd,de->bse', h, w_q).astype(jnp.float32).reshape(B, S, PEER_HEADS, 2, PEER_HALF)
    s = jnp.einsum('bshcd,hckd->bshck', q, sub_keys.astype(jnp.float32))
    top_s, top_i = lax.top_k(s, K)
    cand = (top_s[..., 0, :, None] + top_s[..., 1, None, :]).reshape(B, S, PEER_HEADS, K * K)
    best_s, best_c = lax.top_k(cand, K)
    i1 = jnp.take_along_axis(top_i[..., 0, :], best_c // K, axis=-1)
    i2 = jnp.take_along_axis(top_i[..., 1, :], best_c % K, axis=-1)
    ids = i1 * PEER_NKEYS + i2
    g = jax.nn.softmax(best_s, axis=-1)
    nb = S // TOKEN_BLOCK

    def to_blocks(a):
        return a.reshape(B, nb, TOKEN_BLOCK, *a.shape[2:]).swapaxes(0, 1)

    def block(args):
        xb, idb, gb = args
        act = jax.nn.gelu(jnp.einsum('btd,bthkd->bthk', xb, u_tab[idb]).astype(jnp.float32),
                          approximate=False)
        w = (gb * act).astype(v_tab.dtype)
        return jnp.einsum('bthk,bthkd->btd', w, v_tab[idb])

    out = lax.map(block, (to_blocks(h), to_blocks(ids), to_blocks(g)))
    return out.swapaxes(0, 1).reshape(B, S, D).astype(h.dtype)


def setup_inputs(seed: int = 0) -> dict:
    key = jax.random.key(seed)
    ks = jax.random.split(key, 32)
    f32 = jnp.float32
    n_even = (DEPTH + 1) // 2
    n_odd = DEPTH // 2

    def nrm(k, shape, scale):
        return scale * jax.random.normal(k, shape, f32)

    def gain(k, shape):
        return 1.0 + 0.02 * jax.random.normal(k, shape, f32)

    a0 = jax.random.uniform(ks[16], (n_odd, D_RNN), f32, 0.9, 0.999)
    return {
        'x': nrm(ks[0], (BATCH, SEQ, D_MODEL), 1.0),
        'even_norm_g': gain(ks[1], (n_even, D_MODEL)),
        'even_w_in': nrm(ks[2], (n_even, D_MODEL, EVEN_IN), D_MODEL ** -0.5),
        'even_b_f': 3.0 + nrm(ks[3], (n_even, N_FOX_HEADS), 0.5),
        'even_cmp_pos': nrm(ks[4], (n_even, CMP_LEN, HEAD_DIM), 0.1),
        'even_w_cmp_k': nrm(ks[5], (n_even, CMP_LEN, HEAD_DIM, HEAD_DIM), (CMP_LEN * HEAD_DIM) ** -0.5),
        'even_w_cmp_v': nrm(ks[6], (n_even, CMP_LEN, HEAD_DIM, HEAD_DIM), (CMP_LEN * HEAD_DIM) ** -0.5),
        'even_w_out': nrm(ks[7], (n_even, ATTN_WIDTH, D_MODEL), ATTN_WIDTH ** -0.5),
        'odd_norm_g': gain(ks[8], (n_odd, D_MODEL)),
        'odd_w_in': nrm(ks[9], (n_odd, D_MODEL, 2 * D_RNN), D_MODEL ** -0.5),
        'odd_conv_w': nrm(ks[10], (n_odd, CONV_W, D_RNN), CONV_W ** -0.5),
        'odd_conv_b': nrm(ks[11], (n_odd, D_RNN), 0.01),
        'odd_w_ra': nrm(ks[12], (n_odd, RNN_BLOCKS, RNN_BLOCK_W, RNN_BLOCK_W), RNN_BLOCK_W ** -0.5),
        'odd_b_ra': nrm(ks[13], (n_odd, D_RNN), 0.01),
        'odd_w_ri': nrm(ks[14], (n_odd, RNN_BLOCKS, RNN_BLOCK_W, RNN_BLOCK_W), RNN_BLOCK_W ** -0.5),
        'odd_b_ri': nrm(ks[15], (n_odd, D_RNN), 0.01),
        'odd_lam': jnp.log(a0) - jnp.log1p(-a0),
        'odd_w_out': nrm(ks[17], (n_odd, D_RNN, D_MODEL), D_RNN ** -0.5),
        'ffn_norm_g': gain(ks[18], (DEPTH, D_MODEL)),
        'peer_w_q': nrm(ks[19], (DEPTH, D_MODEL, PEER_HEADS * PEER_DKEY), D_MODEL ** -0.5),
        'peer_sub_keys': nrm(ks[20], (DEPTH, PEER_HEADS, 2, PEER_NKEYS, PEER_HALF), PEER_HALF ** -0.5),
        'peer_u': nrm(ks[21], (DEPTH, PEER_NEXP, D_MODEL), D_MODEL ** -0.5),
        'peer_v': nrm(ks[22], (DEPTH, PEER_NEXP, D_MODEL), PEER_HEADS ** -0.5),
        'final_g': gain(ks[23], (D_MODEL,)),
    }


def reference(x, even_norm_g, even_w_in, even_b_f, even_cmp_pos, even_w_cmp_k, even_w_cmp_v, even_w_out,
              odd_norm_g, odd_w_in, odd_conv_w, odd_conv_b, odd_w_ra, odd_b_ra, odd_w_ri, odd_b_ri,
              odd_lam, odd_w_out, ffn_norm_g, peer_w_q, peer_sub_keys, peer_u, peer_v, final_g):
    h = x
    for layer in range(DEPTH):
        j = layer // 2
        if layer % 2 == 0:
            h = h + even_mixer(rmsnorm(h, even_norm_g[j]), even_w_in[j], even_b_f[j], even_cmp_pos[j],
                               even_w_cmp_k[j], even_w_cmp_v[j], even_w_out[j])
        else:
            h = h + odd_mixer(rmsnorm(h, odd_norm_g[j]), odd_w_in[j], odd_conv_w[j], odd_conv_b[j],
                              odd_w_ra[j], odd_b_ra[j], odd_w_ri[j], odd_b_ri[j], odd_lam[j], odd_w_out[j])
        h = h + peer_ffn(rmsnorm(h, ffn_norm_g[layer]), peer_w_q[layer], peer_sub_keys[layer],
                         peer_u[layer], peer_v[layer])
    return rmsnorm(h, final_g)
```

```python
import functools
import math

import jax
import jax.numpy as jnp
import numpy as np
from jax import lax
from jax.experimental import pallas as pl
from jax.experimental.pallas import tpu as pltpu

HEAD_DIM = 64
N_FOX_HEADS = 8
N_NSA_HEADS = 8
N_NSA_KV = 2
NSA_GROUP = N_NSA_HEADS // N_NSA_KV
CMP_LEN = 32
CMP_STRIDE = 16
SLC_LEN = 64
SLC_TOPK = 16
WINDOW = 512
FORCE_BONUS = 1.0e4
NEG = -1.0e30
D_RNN = 1280
RNN_BLOCKS = 10
RNN_BLOCK_W = D_RNN // RNN_BLOCKS
CONV_W = 4
LRU_C = 8.0
PEER_HEADS = 8
PEER_NKEYS = 128
PEER_HALF = 128
PEER_TOPK = 16
RMS_EPS = 1e-6

LANES = 128
SUBLANES = 8
VMEM_LIMIT = 48 * 1024 * 1024

F32 = jnp.float32
BF16 = jnp.bfloat16


def _cparams(*sem):
    return pltpu.CompilerParams(dimension_semantics=sem, vmem_limit_bytes=VMEM_LIMIT)


def _dot(a, b):
    return lax.dot_general(a, b, (((1,), (0,)), ((), ())), preferred_element_type=F32)


def _dot_nt(a, b):
    return lax.dot_general(a, b, (((1,), (1,)), ((), ())), preferred_element_type=F32)


def _rmsnorm_rows(x, g):
    return x * lax.rsqrt(jnp.mean(x * x, axis=-1, keepdims=True) + RMS_EPS) * g


def _gelu(x):
    return 0.5 * x * (1.0 + lax.erf(x * (1.0 / math.sqrt(2.0))))


def _sigmoid(x):
    return 1.0 / (1.0 + jnp.exp(-x))


def _norm_proj_kernel(x_ref, g_ref, w_ref, *out_refs, segs):
    xn = _rmsnorm_rows(x_ref[...], g_ref[...]).astype(BF16)
    for o_ref, (start, width, scale) in zip(out_refs, segs):
        if start is None:
            o_ref[...] = xn
            continue
        y = _dot(xn, w_ref[:, start:start + width])
        if scale != 1.0:
            y = y * scale
        o_ref[...] = y.astype(o_ref.dtype)


def norm_proj(x2d, g, w_bf16, segs, dtypes, tm=256, name="norm_proj"):
    n, d = x2d.shape
    outs = [jax.ShapeDtypeStruct((n, wd), dt) for (_, wd, _), dt in zip(segs, dtypes)]
    return pl.pallas_call(
        functools.partial(_norm_proj_kernel, segs=tuple(segs)),
        out_shape=outs,
        grid=(n // tm,),
        in_specs=[pl.BlockSpec((tm, d), lambda i: (i, 0)),
                  pl.BlockSpec((1, d), lambda i: (0, 0)),
                  pl.BlockSpec(w_bf16.shape, lambda i: (0, 0))],
        out_specs=[pl.BlockSpec((tm, wd), lambda i: (i, 0)) for (_, wd, _) in segs],
        compiler_params=_cparams("parallel"),
        name=name,
    )(x2d, g.reshape(1, d), w_bf16)


def _proj_residual_kernel(*refs):
    r_ref, o_ref = refs[-2], refs[-1]
    acc = r_ref[...]
    for a_ref, w_ref in zip(refs[0:-2:2], refs[1:-2:2]):
        acc = acc + _dot(a_ref[...], w_ref[...])
    o_ref[...] = acc


def proj_residual(pairs, resid, tm=512, name="proj_residual"):
    n, d = resid.shape
    args, specs = [], []
    for a, w in pairs:
        args += [a, w]
        specs += [pl.BlockSpec((tm, a.shape[1]), lambda i: (i, 0)), pl.BlockSpec(w.shape, lambda i: (0, 0))]
    return pl.pallas_call(
        _proj_residual_kernel,
        out_shape=jax.ShapeDtypeStruct((n, d), F32),
        grid=(n // tm,),
        in_specs=specs + [pl.BlockSpec((tm, d), lambda i: (i, 0))],
        out_specs=pl.BlockSpec((tm, d), lambda i: (i, 0)),
        compiler_params=_cparams("parallel"),
        name=name,
    )(*args, resid)


def _fox_gate_kernel(f_ref, b_ref, c_ref):
    z = f_ref[...] + b_ref[...]
    x = jnp.minimum(z, 0.0) - jnp.log(1.0 + jnp.exp(-jnp.abs(z)))
    s = x.shape[-1]
    lane = lax.broadcasted_iota(jnp.int32, x.shape, 1)
    shift = 1
    while shift < s:
        x = x + jnp.where(lane >= shift, pltpu.roll(x, shift, 1), 0.0)
        shift *= 2
    c_ref[...] = x


def fox_gate_cumsum(f_t, b_f):
    b, h, s = f_t.shape
    return pl.pallas_call(
        _fox_gate_kernel,
        out_shape=jax.ShapeDtypeStruct((b, h, s), F32),
        grid=(b,),
        in_specs=[pl.BlockSpec((None, h, s), lambda i: (i, 0, 0)),
                  pl.BlockSpec((h, 1), lambda i: (0, 0))],
        out_specs=pl.BlockSpec((None, h, s), lambda i: (i, 0, 0)),
        compiler_params=_cparams("parallel"),
        name="fox_gate_cumsum",
    )(f_t, b_f.reshape(h, 1).astype(F32))


def _fox_attn_kernel(q_ref, k_ref, v_ref, cc_ref, cr_ref, o_ref, m_ref, l_ref, acc_ref, *, tq, tk):
    i = pl.program_id(2)
    j = pl.program_id(3)

    @pl.when(j == 0)
    def _():
        m_ref[...] = jnp.full(m_ref.shape, -jnp.inf, F32)
        l_ref[...] = jnp.zeros(l_ref.shape, F32)
        acc_ref[...] = jnp.zeros(acc_ref.shape, F32)

    @pl.when(j <= i)
    def _():
        k = k_ref[...]
        v = v_ref[...]
        row = i * tq + lax.broadcasted_iota(jnp.int32, (tq, tk), 0)
        col = j * tk + lax.broadcasted_iota(jnp.int32, (tq, tk), 1)
        causal = col <= row
        for hh in range(2):
            q = q_ref[:, hh * LANES:(hh + 1) * LANES]
            s = _dot_nt(q, k) + (cc_ref[:, hh:hh + 1] - cr_ref[hh:hh + 1, :])
            s = jnp.where(causal, s, NEG)
            m_old = m_ref[hh]
            m_new = jnp.maximum(m_old, jnp.max(s, axis=-1, keepdims=True))
            alpha = jnp.exp(m_old - m_new)
            p = jnp.exp(s - m_new)
            l_ref[hh] = alpha * l_ref[hh] + jnp.sum(p, axis=-1, keepdims=True)
            acc_ref[hh] = alpha * acc_ref[hh] + _dot(p.astype(BF16), v)
            m_ref[hh] = m_new

    @pl.when(j == pl.num_programs(3) - 1)
    def _():
        lane = lax.broadcasted_iota(jnp.int32, (tq, LANES), 1)
        o0 = acc_ref[0] / l_ref[0]
        o1 = acc_ref[1] / l_ref[1]
        o_ref[...] = jnp.where(lane < HEAD_DIM, o0, o1).astype(o_ref.dtype)


def fox_attention(fq, fk, fv, c, tq=512, tk=512):
    b, s, _ = fk.shape
    hp = N_FOX_HEADS // 2
    cc = c.reshape(b, hp, 2, s).transpose(0, 1, 3, 2)
    cr = c.reshape(b, hp, 2, s)
    nq, nk = s // tq, s // tk
    kv_idx = lambda bb, p, i, j: (bb, jnp.minimum(j, i), p)
    return pl.pallas_call(
        functools.partial(_fox_attn_kernel, tq=tq, tk=tk),
        out_shape=jax.ShapeDtypeStruct((b, s, N_FOX_HEADS * HEAD_DIM), BF16),
        grid=(b, hp, nq, nk),
        in_specs=[pl.BlockSpec((None, tq, 2 * LANES), lambda bb, p, i, j: (bb, i, p)),
                  pl.BlockSpec((None, tk, LANES), kv_idx),
                  pl.BlockSpec((None, tk, LANES), kv_idx),
                  pl.BlockSpec((None, None, tq, 2), lambda bb, p, i, j: (bb, p, i, 0)),
                  pl.BlockSpec((None, None, 2, tk), lambda bb, p, i, j: (bb, p, 0, jnp.minimum(j, i)))],
        out_specs=pl.BlockSpec((None, tq, LANES), lambda bb, p, i, j: (bb, i, p)),
        scratch_shapes=[pltpu.VMEM((2, tq, 1), F32), pltpu.VMEM((2, tq, 1), F32),
                        pltpu.VMEM((2, tq, LANES), F32)],
        compiler_params=_cparams("parallel", "parallel", "parallel", "arbitrary"),
        name="fox_attention",
    )(fq, fk, fv, cc, cr)


FOX_QKV = 3 * N_FOX_HEADS * HEAD_DIM
NSA_Q = N_NSA_HEADS * HEAD_DIM
NSA_KV = 6 * N_NSA_KV * HEAD_DIM
NSA_GATES = 3 * N_NSA_HEADS
_QK_SCALE = HEAD_DIM ** -0.5

_EVEN_SEGS = ((0, 1024, _QK_SCALE), (1024, 512, 1.0), (1536, 512, 1.0), (2048, 1024, _QK_SCALE),
              (3072, 256, 1.0), (3328, 512, 1.0), (3840, 128, 1.0))
_EVEN_DTYPES = (BF16, BF16, BF16, BF16, F32, BF16, F32)


def _even_w_ext(w_in):
    d = w_in.shape[0]
    o = 0
    w_fq = w_in[:, o:o + 512].reshape(d, N_FOX_HEADS, HEAD_DIM); o += 512
    w_fk = w_in[:, o:o + 512]; o += 512
    w_fv = w_in[:, o:o + 512]; o += 512
    w_ff = w_in[:, o:o + N_FOX_HEADS]; o += N_FOX_HEADS
    w_nq = w_in[:, o:o + NSA_Q].reshape(d, N_NSA_HEADS, HEAD_DIM); o += NSA_Q
    w_nkv = w_in[:, o:o + NSA_KV]; o += NSA_KV
    w_ng = w_in[:, o:o + NSA_GATES]
    half_f = (np.arange(N_FOX_HEADS) % 2)[None, :, None, None] == np.arange(2)[None, None, :, None]
    fq = jnp.where(half_f, w_fq[:, :, None, :], 0.0).reshape(d, N_FOX_HEADS * LANES)
    half_n = (np.arange(N_NSA_HEADS) // NSA_GROUP)[None, :, None, None] == np.arange(2)[None, None, :, None]
    nq = jnp.where(half_n, w_nq[:, :, None, :], 0.0).reshape(d, N_NSA_HEADS * LANES)
    small = jnp.concatenate([w_ff, w_ng, jnp.zeros((d, LANES - N_FOX_HEADS - NSA_GATES), w_in.dtype)], axis=1)
    return jnp.concatenate([fq, w_fk, w_fv, nq, w_nkv[:, :256], w_nkv[:, 256:], small], axis=1).astype(BF16)


def _nsa_compress_kernel(x_ref, pos_ref, w_ref, o_ref):
    n = x_ref.shape[2]
    for kind in range(2):
        acc = jnp.zeros((n, LANES), F32)
        for g in range(N_NSA_KV):
            x = x_ref[kind, g]
            top = _dot((x + pos_ref[0:1, :]).astype(BF16), w_ref[kind, g, 0])
            bot = _dot((x + pos_ref[1:2, :]).astype(BF16), w_ref[kind, g, 1])
            acc = acc + top + pltpu.roll(bot, n - 1, 0)
        o_ref[kind] = acc.astype(o_ref.dtype)


def nsa_compress(ncmp, cmp_pos, w_cmp_k, w_cmp_v):
    b, s, _ = ncmp.shape
    n = s // CMP_STRIDE
    half = CMP_STRIDE * HEAD_DIM
    x = ncmp.reshape(b, n, CMP_STRIDE, 2, N_NSA_KV, HEAD_DIM).transpose(0, 3, 4, 1, 2, 5).reshape(b, 2, N_NSA_KV, n, half)
    pos = cmp_pos.astype(F32).reshape(2, half)
    w = jnp.stack([w_cmp_k, w_cmp_v]).astype(F32).reshape(2, 2, half, HEAD_DIM)
    place = (np.arange(N_NSA_KV)[:, None, None] == np.arange(2)[None, :, None])
    w_pad = jnp.where(place[None, :, None, None], w[:, None, :, :, None, :], 0.0)
    w_pad = w_pad.reshape(2, N_NSA_KV, 2, half, LANES).astype(BF16)
    return pl.pallas_call(
        _nsa_compress_kernel,
        out_shape=jax.ShapeDtypeStruct((b, 2, n, LANES), BF16),
        grid=(b,),
        in_specs=[pl.BlockSpec((None, 2, N_NSA_KV, n, half), lambda i: (i, 0, 0, 0, 0)),
                  pl.BlockSpec((2, half), lambda i: (0, 0)),
                  pl.BlockSpec((2, N_NSA_KV, 2, half, LANES), lambda i: (0, 0, 0, 0, 0))],
        out_specs=pl.BlockSpec((None, 2, n, LANES), lambda i: (i, 0, 0, 0)),
        compiler_params=_cparams("parallel"),
        name="nsa_compress",
    )(x, pos, w_pad)


def _nsa_attn_kernel(q_ref, kvc_ref, kv_ref, gate_ref, slope_ref, ovt_ref, et_ref, o_ref,
                     q4_ref, m_ref, l_ref, acc_ref, osel_ref, *, tq):
    g = pl.program_id(1)
    i = pl.program_id(2)
    t0 = i * tq
    n_cmp = kvc_ref.shape[1]
    R = NSA_GROUP

    for r in range(R):
        q4_ref[r * tq:(r + 1) * tq, :] = q_ref[:, r * LANES:(r + 1) * LANES]
    q4 = q4_ref[...]

    t_col = t0 + lax.broadcasted_iota(jnp.int32, (tq, n_cmp), 0)
    cend = CMP_STRIDE * lax.broadcasted_iota(jnp.int32, (tq, n_cmp), 1) + (CMP_LEN - 1)
    cmask = cend <= t_col
    cdist = (t_col - cend).astype(F32)
    any_c = (t_col[:, 0:1] >= CMP_LEN - 1).astype(F32)
    s_c = _dot_nt(q4, kvc_ref[0])
    imp_t = jnp.zeros((LANES, tq), F32)
    o_cmp = []
    for r in range(R):
        s = s_c[r * tq:(r + 1) * tq] - slope_ref[r:r + 1, 0:1] * cdist
        s = jnp.where(cmask, s, NEG)
        e = jnp.exp(s - jnp.max(s, axis=-1, keepdims=True))
        p = (e / jnp.sum(e, axis=-1, keepdims=True)) * any_c
        pb = p.astype(BF16)
        o_cmp.append(_dot(pb, kvc_ref[1]))
        imp_t = imp_t + _dot_nt(ovt_ref[...], pb)

    jrow = lax.broadcasted_iota(jnp.int32, (LANES, tq), 0)
    cur = (t0 + lax.broadcasted_iota(jnp.int32, (LANES, tq), 1)) // SLC_LEN
    forced = ((jrow == 0) | (jrow == cur) | (jrow == cur - 1)).astype(F32)
    imp_t = jnp.where(jrow <= cur, imp_t + FORCE_BONUS * forced, -jnp.inf)
    n_sel = et_ref.shape[0] // SLC_LEN
    rank = jnp.zeros((LANES, tq), jnp.int32)
    for jp in range(n_sel):
        other = imp_t[jp:jp + 1, :]
        ahead = (other > imp_t) | ((other == imp_t) & (jp < jrow))
        rank = rank + ahead.astype(jnp.int32)
    sel_t = (rank < SLC_TOPK).astype(F32)
    sel = sel_t.T.astype(BF16)

    row_t = t0 + lax.broadcasted_iota(jnp.int32, (tq, tq), 0)
    col_in = lax.broadcasted_iota(jnp.int32, (tq, tq), 1)

    def attend(lo, hi, lane_blk, use_sel):
        m_ref[...] = jnp.full(m_ref.shape, -jnp.inf, F32)
        l_ref[...] = jnp.zeros(l_ref.shape, F32)
        acc_ref[...] = jnp.zeros(acc_ref.shape, F32)

        def body(j, carry):
            start = pl.multiple_of(j * tq, tq)
            k = kv_ref[pl.ds(start, tq), lane_blk * LANES:(lane_blk + 1) * LANES]
            v = kv_ref[pl.ds(start, tq), (lane_blk + 1) * LANES:(lane_blk + 2) * LANES]
            dist_i = row_t - (start + col_in)
            dist = dist_i.astype(F32)
            if use_sel:
                chosen = _dot_nt(sel, et_ref[pl.ds(start, tq), :])
                mask = (dist_i >= 0) & (chosen > 0.5)
            else:
                mask = (dist_i >= 0) & (dist_i < WINDOW)
            s4 = _dot_nt(q4, k)
            for r in range(R):
                s = s4[r * tq:(r + 1) * tq] - slope_ref[r:r + 1, 0:1] * dist
                s = jnp.where(mask, s, NEG)
                m_old = m_ref[r]
                m_new = jnp.maximum(m_old, jnp.max(s, axis=-1, keepdims=True))
                alpha = jnp.exp(m_old - m_new)
                p = jnp.exp(s - m_new)
                l_ref[r] = alpha * l_ref[r] + jnp.sum(p, axis=-1, keepdims=True)
                acc_ref[r] = alpha * acc_ref[r] + _dot(p.astype(BF16), v)
                m_ref[r] = m_new
            return carry

        lax.fori_loop(lo, hi, body, 0)

    attend(0, i + 1, 0, True)
    for r in range(R):
        osel_ref[r] = acc_ref[r] / l_ref[r]
    attend(jnp.maximum(i - WINDOW // tq, 0), i + 1, 2, False)

    gates = _sigmoid(gate_ref[...])
    lane = lax.broadcasted_iota(jnp.int32, (tq, LANES), 1)
    mine = (lane // HEAD_DIM) == g
    for r in range(R):
        o = (gates[:, 3 * r:3 * r + 1] * o_cmp[r] + gates[:, 3 * r + 1:3 * r + 2] * osel_ref[r]
             + gates[:, 3 * r + 2:3 * r + 3] * (acc_ref[r] / l_ref[r]))
        o_ref[:, r * LANES:(r + 1) * LANES] = jnp.where(mine, o, 0.0).astype(o_ref.dtype)


def _alibi_slopes(n):
    return np.asarray(2.0 ** (-8.0 * np.arange(1, n + 1) / n), np.float32)


def nsa_attention(nq, kvc, nrest, gates_g, tq=128):
    b, s, _ = nrest.shape
    n_cmp = s // CMP_STRIDE
    n_sel = s // SLC_LEN
    slopes = np.broadcast_to(_alibi_slopes(N_NSA_HEADS).reshape(N_NSA_KV, NSA_GROUP, 1), (N_NSA_KV, NSA_GROUP, LANES))
    slopes = jnp.asarray(np.concatenate([slopes, np.zeros((N_NSA_KV, SUBLANES - NSA_GROUP, LANES), np.float32)], axis=1))
    cstart = CMP_STRIDE * np.arange(n_cmp)
    sstart = SLC_LEN * np.arange(LANES)
    ov = ((cstart[None, :] < sstart[:, None] + SLC_LEN) & (cstart[None, :] + CMP_LEN > sstart[:, None])
          & (np.arange(LANES)[:, None] < n_sel) & (cstart[None, :] + CMP_LEN <= s))
    ovt = jnp.asarray(ov, BF16)
    et = jnp.asarray((np.arange(s)[:, None] // SLC_LEN) == np.arange(LANES)[None, :], BF16)
    return pl.pallas_call(
        functools.partial(_nsa_attn_kernel, tq=tq),
        out_shape=jax.ShapeDtypeStruct((b, s, N_NSA_HEADS * LANES), BF16),
        grid=(b, N_NSA_KV, s // tq),
        in_specs=[pl.BlockSpec((None, tq, NSA_GROUP * LANES), lambda bb, g, i: (bb, i, g)),
                  pl.BlockSpec((None, 2, n_cmp, LANES), lambda bb, g, i: (bb, 0, 0, 0)),
                  pl.BlockSpec((None, s, 4 * LANES), lambda bb, g, i: (bb, 0, 0)),
                  pl.BlockSpec((None, None, tq, LANES), lambda bb, g, i: (bb, g, i, 0)),
                  pl.BlockSpec((None, SUBLANES, LANES), lambda bb, g, i: (g, 0, 0)),
                  pl.BlockSpec((LANES, n_cmp), lambda bb, g, i: (0, 0)),
                  pl.BlockSpec((s, LANES), lambda bb, g, i: (0, 0))],
        out_specs=pl.BlockSpec((None, tq, NSA_GROUP * LANES), lambda bb, g, i: (bb, i, g)),
        scratch_shapes=[pltpu.VMEM((NSA_GROUP * tq, LANES), BF16),
                        pltpu.VMEM((NSA_GROUP, tq, 1), F32), pltpu.VMEM((NSA_GROUP, tq, 1), F32),
                        pltpu.VMEM((NSA_GROUP, tq, LANES), F32), pltpu.VMEM((NSA_GROUP, tq, LANES), F32)],
        compiler_params=_cparams("parallel", "parallel", "arbitrary"),
        name="nsa_attention",
    )(nq, kvc, nrest, gates_g, slopes, ovt, et)


def _peer_cand_layout():
    K = PEER_TOPK
    groups = [((0, 1), (0, K))]
    groups += [((j1, j1 + 1), (0, SUBLANES)) for j1 in range(1, SUBLANES)]
    groups += [((SUBLANES, K), (0, 1))]
    flat = []
    for (a0, a1), (b0, b1) in groups:
        flat += [j1 * K + j2 for j1 in range(a0, a1) for j2 in range(b0, b1)]
    return groups, np.asarray(flat, np.int32)


def _extract_top(vals, tags, n, write):
    big = jnp.int32(2 ** 30)
    for it in range(n):
        m = jnp.max(vals, axis=0, keepdims=True)
        tag = jnp.min(jnp.where(vals == m, tags, big), axis=0, keepdims=True)
        write(it, m, tag)
        vals = jnp.where(tags == tag, -jnp.inf, vals)


def _peer_topk_kernel(q_ref, keys_ref, flat_ref, a_ref, b_ref, g_ref,
                      s1_ref, s2_ref, i1_ref, i2_ref, bs_ref, bid_ref, at_ref, bt_ref, gt_ref, *, tm, groups):
    K = PEER_TOPK
    kidx = lax.broadcasted_iota(jnp.int32, (PEER_NKEYS, tm), 0)
    flat = flat_ref[...]
    for h in range(PEER_HEADS):
        for c, (s_ref, i_ref) in enumerate(((s1_ref, i1_ref), (s2_ref, i2_ref))):
            blk = (2 * h + c) * LANES
            scores = _dot_nt(keys_ref[h, c], q_ref[:, blk:blk + LANES])

            def write(it, m, tag, s_ref=s_ref, i_ref=i_ref):
                s_ref[it:it + 1, :] = m
                i_ref[it:it + 1, :] = tag

            _extract_top(scores, kidx, K, write)
        s1, s2, i1, i2 = s1_ref[...], s2_ref[...], i1_ref[...], i2_ref[...]
        cand, ids = [], []
        for (a0, a1), (b0, b1) in groups:
            cand.append(s1[a0:a1] + s2[b0:b1])
            ids.append(i1[a0:a1] * PEER_NKEYS + i2[b0:b1])
        cand = jnp.concatenate(cand, axis=0)
        ids = jnp.concatenate(ids, axis=0)

        def write2(it, m, tag):
            bs_ref[it:it + 1, :] = m
            bid_ref[it:it + 1, :] = jnp.max(jnp.where(flat == tag, ids, -1), axis=0, keepdims=True)

        _extract_top(cand, flat, K, write2)
        bs = bs_ref[...]
        e = jnp.exp(bs - jnp.max(bs, axis=0, keepdims=True))
        gt_ref[h * K:(h + 1) * K, :] = e / jnp.sum(e, axis=0, keepdims=True)
        bid = bid_ref[...]
        at_ref[h * K:(h + 1) * K, :] = bid // PEER_NKEYS
        bt_ref[h * K:(h + 1) * K, :] = bid % PEER_NKEYS
    for blk in range(tm // LANES):
        sl = slice(blk * LANES, (blk + 1) * LANES)
        a_ref[sl, :] = at_ref[:, sl].T
        b_ref[sl, :] = bt_ref[:, sl].T
        g_ref[sl, :] = gt_ref[:, sl].T


def peer_topk(q_bf16, sub_keys, tm=256):
    n = q_bf16.shape[0]
    groups, flat = _peer_cand_layout()
    nc = flat.shape[0]
    flat_b = jnp.asarray(np.broadcast_to(flat[:, None], (nc, tm)))
    hk = PEER_HEADS * PEER_TOPK
    row = lambda dt: jax.ShapeDtypeStruct((n, hk), dt)
    return pl.pallas_call(
        functools.partial(_peer_topk_kernel, tm=tm, groups=groups),
        out_shape=[row(jnp.int32), row(jnp.int32), row(F32)],
        grid=(n // tm,),
        in_specs=[pl.BlockSpec((tm, q_bf16.shape[1]), lambda i: (i, 0)),
                  pl.BlockSpec(sub_keys.shape, lambda i: (0, 0, 0, 0)),
                  pl.BlockSpec((nc, tm), lambda i: (0, 0))],
        out_specs=[pl.BlockSpec((tm, hk), lambda i: (i, 0))] * 3,
        scratch_shapes=[pltpu.VMEM((PEER_TOPK, tm), F32), pltpu.VMEM((PEER_TOPK, tm), F32),
                        pltpu.VMEM((PEER_TOPK, tm), jnp.int32), pltpu.VMEM((PEER_TOPK, tm), jnp.int32),
                        pltpu.VMEM((PEER_TOPK, tm), F32), pltpu.VMEM((PEER_TOPK, tm), jnp.int32),
                        pltpu.VMEM((hk, tm), jnp.int32), pltpu.VMEM((hk, tm), jnp.int32),
                        pltpu.VMEM((hk, tm), F32)],
        compiler_params=_cparams("parallel"),
        name="peer_topk",
    )(q_bf16, sub_keys.astype(BF16), flat_b)


_W_PITCH = PEER_NKEYS + SUBLANES


def _peer_apply_kernel(x_ref, u_ref, v_ref, a_ref, b_ref, g_ref, r_ref, o_ref, w_ref, *, tm, ec, unroll):
    c = pl.program_id(1)
    slabs = ec // PEER_NKEYS

    @pl.when(c == 0)
    def _():
        o_ref[...] = r_ref[...]
        sub = lax.broadcasted_iota(jnp.int32, (PEER_NKEYS, LANES), 0)

        def build(tb, carry):
            for tt in range(unroll):
                t = tb * unroll + tt
                a = a_ref[pl.ds(t, 1), :]
                b = b_ref[pl.ds(t, 1), :]
                g = g_ref[pl.ds(t, 1), :]
                pt = jnp.where(sub == a, g, 0.0).astype(BF16)
                qt = jnp.where(sub == b, 1.0, 0.0).astype(BF16)
                w_ref[pl.ds(pl.multiple_of(t * _W_PITCH, SUBLANES), PEER_NKEYS), :] = _dot_nt(pt, qt)
            return carry

        lax.fori_loop(0, tm // unroll, build, 0)

    act = _dot_nt(x_ref[...], u_ref[...])
    w = jnp.concatenate([w_ref[pl.ds(c * slabs + s, tm, stride=_W_PITCH), :] for s in range(slabs)], axis=1)
    o_ref[...] += _dot((_gelu(act) * w).astype(BF16), v_ref[...])


def peer_apply(xn_bf16, u_bf16, v_bf16, a_idx, b_idx, gw, resid, tm=256, ec=2048, unroll=8):
    n, d = xn_bf16.shape
    ne = u_bf16.shape[0]
    tok = lambda w: pl.BlockSpec((tm, w), lambda i, c: (i, 0))
    return pl.pallas_call(
        functools.partial(_peer_apply_kernel, tm=tm, ec=ec, unroll=unroll),
        out_shape=jax.ShapeDtypeStruct((n, d), F32),
        grid=(n // tm, ne // ec),
        in_specs=[tok(d),
                  pl.BlockSpec((ec, d), lambda i, c: (c, 0)),
                  pl.BlockSpec((ec, d), lambda i, c: (c, 0)),
                  tok(a_idx.shape[1]), tok(b_idx.shape[1]), tok(gw.shape[1]), tok(d)],
        out_specs=tok(d),
        scratch_shapes=[pltpu.VMEM((tm * _W_PITCH, LANES), F32)],
        compiler_params=pltpu.CompilerParams(dimension_semantics=("parallel", "arbitrary"),
                                             vmem_limit_bytes=56 * 1024 * 1024),
        name="peer_apply",
    )(xn_bf16, u_bf16, v_bf16, a_idx, b_idx, gw, resid)


def _odd_mixer_kernel(h_ref, g_ref, win_ref, cw_ref, cb_ref, wra_ref, bra_ref, wri_ref, bri_ref, lam_ref, wout_ref,
                      o_ref, xbuf_ref, carry_ref, *, tc):
    j = pl.program_id(1)
    d_rnn = cw_ref.shape[1]
    pad = SUBLANES

    @pl.when(j == 0)
    def _():
        xbuf_ref[0:pad, :] = jnp.zeros((pad, d_rnn), F32)
        carry_ref[...] = jnp.zeros(carry_ref.shape, F32)

    h_in = h_ref[...]
    z = _dot(_rmsnorm_rows(h_in, g_ref[...]).astype(BF16), win_ref[...])
    gate = z[:, :d_rnn]
    xbuf_ref[pad:pad + tc, :] = z[:, d_rnn:]
    xc = cb_ref[...] + cw_ref[0:1, :] * xbuf_ref[pl.ds(pad - 3, tc), :]
    for w in range(1, CONV_W):
        xc = xc + cw_ref[w:w + 1, :] * xbuf_ref[pl.ds(pad - 3 + w, tc), :]
    xbuf_ref[0:pad, :] = xbuf_ref[tc:tc + pad, :]

    xcb = xc.astype(BF16)
    nb = d_rnn // RNN_BLOCK_W
    blk = lambda w_ref: jnp.concatenate(
        [_dot(xcb[:, n * RNN_BLOCK_W:(n + 1) * RNN_BLOCK_W], w_ref[n]) for n in range(nb)], axis=1)
    r = _sigmoid(blk(wra_ref) + bra_ref[...])
    gi = _sigmoid(blk(wri_ref) + bri_ref[...])
    nl = -lam_ref[...]
    softplus = jnp.maximum(nl, 0.0) + jnp.log(1.0 + jnp.exp(-jnp.abs(nl)))
    log_a = (-LRU_C) * r * softplus
    a = jnp.exp(log_a)
    u = jnp.sqrt(1.0 - jnp.exp(2.0 * log_a)) * (gi * xc)

    row = lax.broadcasted_iota(jnp.int32, (tc, d_rnn), 0)
    shift = 1
    while shift < tc:
        valid = row >= shift
        a_prev = jnp.where(valid, pltpu.roll(a, shift, 0), 1.0)
        u_prev = jnp.where(valid, pltpu.roll(u, shift, 0), 0.0)
        u = u + a * u_prev
        a = a * a_prev
        shift *= 2
    hs = u + a * carry_ref[...]
    carry_ref[...] = hs[tc - 1:tc, :]

    y = (_gelu(gate) * hs).astype(BF16)
    o_ref[...] = h_in + _dot(y, wout_ref[...])


def odd_mixer_residual(h, norm_g, w_in, conv_w, conv_b, w_ra, b_ra, w_ri, b_ri, lam, w_out, tc=256):
    b, s, d = h.shape
    d_rnn = conv_w.shape[1]
    full = lambda a: pl.BlockSpec(a.shape, lambda bb, j, nd=a.ndim: (0,) * nd)
    row = lambda v: v.astype(F32).reshape(1, -1)
    args = [norm_g.astype(F32).reshape(1, d), w_in.astype(BF16), conv_w.astype(F32), row(conv_b),
            w_ra.astype(BF16), row(b_ra), w_ri.astype(BF16), row(b_ri), row(lam), w_out.astype(BF16)]
    return pl.pallas_call(
        functools.partial(_odd_mixer_kernel, tc=tc),
        out_shape=jax.ShapeDtypeStruct((b, s, d), F32),
        grid=(b, s // tc),
        in_specs=[pl.BlockSpec((None, tc, d), lambda bb, j: (bb, j, 0))] + [full(a) for a in args],
        out_specs=pl.BlockSpec((None, tc, d), lambda bb, j: (bb, j, 0)),
        scratch_shapes=[pltpu.VMEM((tc + 2 * SUBLANES, d_rnn), F32), pltpu.VMEM((1, d_rnn), F32)],
        compiler_params=_cparams("parallel", "arbitrary"),
        name="odd_mixer",
    )(h, *args)


def _rmsnorm_kernel(x_ref, g_ref, o_ref):
    o_ref[...] = _rmsnorm_rows(x_ref[...], g_ref[...])


def rmsnorm_rows(x2d, g, tm=512):
    n, d = x2d.shape
    return pl.pallas_call(
        _rmsnorm_kernel,
        out_shape=jax.ShapeDtypeStruct((n, d), F32),
        grid=(n // tm,),
        in_specs=[pl.BlockSpec((tm, d), lambda i: (i, 0)), pl.BlockSpec((1, d), lambda i: (0, 0))],
        out_specs=pl.BlockSpec((tm, d), lambda i: (i, 0)),
        compiler_params=_cparams("parallel"),
        name="final_rmsnorm",
    )(x2d, g.astype(F32).reshape(1, d))


def even_mixer_residual(h, norm_g, w_in, b_f, cmp_pos, w_cmp_k, w_cmp_v, w_out):
    b, s, d = h.shape
    h2d = h.reshape(b * s, d)
    outs = norm_proj(h2d, norm_g.astype(F32), _even_w_ext(w_in), _EVEN_SEGS, _EVEN_DTYPES, name="even_in_proj")
    fq, fk, fv, nq, ncmp, nrest, small = [o.reshape(b, s, -1) for o in outs]
    c = fox_gate_cumsum(small[..., :N_FOX_HEADS].transpose(0, 2, 1), b_f)
    o_fox = fox_attention(fq, fk, fv, c)
    kvc = nsa_compress(ncmp, cmp_pos, w_cmp_k, w_cmp_v)
    gates_g = small[..., N_FOX_HEADS:N_FOX_HEADS + NSA_GATES].reshape(b, s, N_NSA_KV, 3 * NSA_GROUP).transpose(0, 2, 1, 3)
    gates_g = jnp.pad(gates_g, ((0, 0), (0, 0), (0, 0), (0, LANES - 3 * NSA_GROUP)))
    o_nsa = nsa_attention(nq, kvc, nrest, gates_g)
    nf = N_FOX_HEADS * HEAD_DIM
    w_n = w_out[nf:].reshape(N_NSA_HEADS, HEAD_DIM, d)
    half_n = (np.arange(N_NSA_HEADS) // NSA_GROUP)[:, None, None, None] == np.arange(2)[None, :, None, None]
    w_n = jnp.where(half_n, w_n[:, None], 0.0).reshape(N_NSA_HEADS * LANES, d)
    pairs = [(o_fox.reshape(b * s, -1), w_out[:nf].astype(BF16)), (o_nsa.reshape(b * s, -1), w_n.astype(BF16))]
    return proj_residual(pairs, h2d, name="even_out_proj").reshape(b, s, d)


def peer_ffn_residual(h2d, norm_g, w_q, sub_keys, u_tab, v_tab):
    d = h2d.shape[1]
    nq = w_q.shape[1]
    q, xn = norm_proj(h2d, norm_g, w_q.astype(BF16), ((0, nq, 1.0), (None, d, 1.0)), (BF16, BF16), name="peer_query")
    a_idx, b_idx, gw = peer_topk(q, sub_keys)
    return peer_apply(xn, u_tab.astype(BF16), v_tab.astype(BF16), a_idx, b_idx, gw, h2d)


def kernel(x, even_norm_g, even_w_in, even_b_f, even_cmp_pos, even_w_cmp_k, even_w_cmp_v, even_w_out, odd_norm_g, odd_w_in, odd_conv_w, odd_conv_b, odd_w_ra, odd_b_ra, odd_w_ri, odd_b_ri, odd_lam, odd_w_out, ffn_norm_g, peer_w_q, peer_sub_keys, peer_u, peer_v, final_g):
    b, s, d = x.shape
    depth = ffn_norm_g.shape[0]
    h = x
    for layer in range(depth):
        j = layer // 2
        if layer % 2 == 0:
            h = even_mixer_residual(h, even_norm_g[j], even_w_in[j], even_b_f[j], even_cmp_pos[j],
                                    even_w_cmp_k[j], even_w_cmp_v[j], even_w_out[j])
        else:
            h = odd_mixer_residual(h, odd_norm_g[j], odd_w_in[j], odd_conv_w[j], odd_conv_b[j], odd_w_ra[j],
                                   odd_b_ra[j], odd_w_ri[j], odd_b_ri[j], odd_lam[j], odd_w_out[j])
        h = peer_ffn_residual(h.reshape(b * s, d), ffn_norm_g[layer].astype(F32), peer_w_q[layer],
                              peer_sub_keys[layer], peer_u[layer], peer_v[layer]).reshape(b, s, d)
    return rmsnorm_rows(h.reshape(b * s, d), final_g).reshape(b, s, d)
```

```python
import functools
import math

import jax
import jax.numpy as jnp
import numpy as np
from jax import lax
from jax.experimental import pallas as pl
from jax.experimental.pallas import tpu as pltpu

HEAD_DIM = 64
N_FOX_HEADS = 8
N_NSA_HEADS = 8
N_NSA_KV = 2
NSA_GROUP = N_NSA_HEADS // N_NSA_KV
CMP_LEN = 32
CMP_STRIDE = 16
SLC_LEN = 64
SLC_TOPK = 16
WINDOW = 512
FORCE_BONUS = 1.0e4
NEG = -1.0e30
D_RNN = 1280
RNN_BLOCKS = 10
RNN_BLOCK_W = D_RNN // RNN_BLOCKS
CONV_W = 4
LRU_C = 8.0
PEER_HEADS = 8
PEER_NKEYS = 128
PEER_HALF = 128
PEER_TOPK = 16
RMS_EPS = 1e-6

LANES = 128
SUBLANES = 8
VMEM_LIMIT = 48 * 1024 * 1024

F32 = jnp.float32
BF16 = jnp.bfloat16


def _cparams(*sem):
    return pltpu.CompilerParams(dimension_semantics=sem, vmem_limit_bytes=VMEM_LIMIT)


def _dot(a, b):
    return lax.dot_general(a, b, (((1,), (0,)), ((), ())), preferred_element_type=F32)


def _dot_nt(a, b):
    return lax.dot_general(a, b, (((1,), (1,)), ((), ())), preferred_element_type=F32)


def _rmsnorm_rows(x, g):
    return x * lax.rsqrt(jnp.mean(x * x, axis=-1, keepdims=True) + RMS_EPS) * g


def _gelu(x):
    return 0.5 * x * (1.0 + lax.erf(x * (1.0 / math.sqrt(2.0))))


def _sigmoid(x):
    return 1.0 / (1.0 + jnp.exp(-x))


def _norm_proj_kernel(x_ref, g_ref, w_ref, wt_ref, *out_refs, segs, tsegs):
    xn = _rmsnorm_rows(x_ref[...], g_ref[...]).astype(BF16)
    for o_ref, (start, width, scale) in zip(out_refs, segs):
        if start is None:
            o_ref[...] = xn
            continue
        y = _dot(xn, w_ref[:, start:start + width])
        if scale != 1.0:
            y = y * scale
        o_ref[...] = y.astype(o_ref.dtype)
    for o_ref, (start, width) in zip(out_refs[len(segs):], tsegs):
        o_ref[...] = _dot_nt(wt_ref[start:start + width, :], xn).astype(o_ref.dtype)


def norm_proj(x2d, g, w_bf16, segs, dtypes, wt_bf16=None, tsegs=(), tm=256, name="norm_proj"):
    n, d = x2d.shape
    if wt_bf16 is None:
        wt_bf16 = jnp.zeros((SUBLANES, d), BF16)
    outs = [jax.ShapeDtypeStruct((n, wd), dt) for (_, wd, _), dt in zip(segs, dtypes)]
    outs += [jax.ShapeDtypeStruct((wd, n), BF16) for (_, wd) in tsegs]
    return pl.pallas_call(
        functools.partial(_norm_proj_kernel, segs=tuple(segs), tsegs=tuple(tsegs)),
        out_shape=outs,
        grid=(n // tm,),
        in_specs=[pl.BlockSpec((tm, d), lambda i: (i, 0)),
                  pl.BlockSpec((1, d), lambda i: (0, 0)),
                  pl.BlockSpec(w_bf16.shape, lambda i: (0, 0)),
                  pl.BlockSpec(wt_bf16.shape, lambda i: (0, 0))],
        out_specs=([pl.BlockSpec((tm, wd), lambda i: (i, 0)) for (_, wd, _) in segs]
                   + [pl.BlockSpec((wd, tm), lambda i: (0, i)) for (_, wd) in tsegs]),
        compiler_params=_cparams("parallel"),
        name=name,
    )(x2d, g.reshape(1, d), w_bf16, wt_bf16)


def _proj_residual_kernel(*refs):
    r_ref, o_ref = refs[-2], refs[-1]
    acc = r_ref[...]
    for a_ref, w_ref in zip(refs[0:-2:2], refs[1:-2:2]):
        acc = acc + _dot(a_ref[...], w_ref[...])
    o_ref[...] = acc


def proj_residual(pairs, resid, tm=512, name="proj_residual"):
    n, d = resid.shape
    args, specs = [], []
    for a, w in pairs:
        args += [a, w]
        specs += [pl.BlockSpec((tm, a.shape[1]), lambda i: (i, 0)), pl.BlockSpec(w.shape, lambda i: (0, 0))]
    return pl.pallas_call(
        _proj_residual_kernel,
        out_shape=jax.ShapeDtypeStruct((n, d), F32),
        grid=(n // tm,),
        in_specs=specs + [pl.BlockSpec((tm, d), lambda i: (i, 0))],
        out_specs=pl.BlockSpec((tm, d), lambda i: (i, 0)),
        compiler_params=_cparams("parallel"),
        name=name,
    )(*args, resid)


def _fox_gate_kernel(f_ref, b_ref, c_ref):
    z = f_ref[...] + b_ref[...]
    x = jnp.minimum(z, 0.0) - jnp.log(1.0 + jnp.exp(-jnp.abs(z)))
    s = x.shape[-1]
    lane = lax.broadcasted_iota(jnp.int32, x.shape, 1)
    shift = 1
    while shift < s:
        x = x + jnp.where(lane >= shift, pltpu.roll(x, shift, 1), 0.0)
        shift *= 2
    c_ref[...] = x


def fox_gate_cumsum(f_t, b_f):
    b, h, s = f_t.shape
    return pl.pallas_call(
        _fox_gate_kernel,
        out_shape=jax.ShapeDtypeStruct((b, h, s), F32),
        grid=(b,),
        in_specs=[pl.BlockSpec((None, h, s), lambda i: (i, 0, 0)),
                  pl.BlockSpec((h, 1), lambda i: (0, 0))],
        out_specs=pl.BlockSpec((None, h, s), lambda i: (i, 0, 0)),
        compiler_params=_cparams("parallel"),
        name="fox_gate_cumsum",
    )(f_t, b_f.reshape(h, 1).astype(F32))


ATTN_KB = 512


def _fox_attn_kernel(q_ref, k_ref, vt_ref, ct_ref, cs_ref, o_ref, qs_ref, m_ref, l_ref, acc_ref, *, tq):
    KB = ATTN_KB
    i = pl.program_id(2)
    t0 = i * tq
    for hh in range(2):
        qs_ref[hh * tq:(hh + 1) * tq, :] = q_ref[:, hh * LANES:(hh + 1) * LANES]
    qs = qs_ref[...]
    m_ref[...] = jnp.full(m_ref.shape, -jnp.inf, F32)
    l_ref[...] = jnp.zeros(l_ref.shape, F32)
    acc_ref[...] = jnp.zeros(acc_ref.shape, F32)
    t_pos = t0 + lax.broadcasted_iota(jnp.int32, (KB, tq), 1)
    s_in = lax.broadcasted_iota(jnp.int32, (KB, tq), 0)

    def step(kb, masked):
        start = pl.multiple_of(kb * KB, KB)
        st = _dot_nt(k_ref[pl.ds(start, KB), :], qs)
        vt = vt_ref[:, pl.ds(start, KB)]
        cs = cs_ref[pl.ds(start, KB), :]
        for hh in range(2):
            s = st[:, hh * tq:(hh + 1) * tq] + (ct_ref[hh:hh + 1, :] - cs[:, hh:hh + 1])
            if masked:
                s = jnp.where(start + s_in <= t_pos, s, NEG)
            m_old = m_ref[hh:hh + 1, :]
            m_new = jnp.maximum(m_old, jnp.max(s, axis=0, keepdims=True))
            alpha = jnp.exp(m_old - m_new)
            p = jnp.exp(s - m_new)
            l_ref[hh:hh + 1, :] = alpha * l_ref[hh:hh + 1, :] + jnp.sum(p, axis=0, keepdims=True)
            acc_ref[hh] = alpha * acc_ref[hh] + _dot(vt, p.astype(BF16))
            m_ref[hh:hh + 1, :] = m_new

    n_full = t0 // KB

    def full_body(kb, carry):
        step(kb, False)
        return carry

    lax.fori_loop(0, n_full, full_body, 0)
    step(n_full, True)

    row = lax.broadcasted_iota(jnp.int32, (LANES, tq), 0)
    o0 = acc_ref[0] / l_ref[0:1, :]
    o1 = acc_ref[1] / l_ref[1:2, :]
    ot = jnp.where(row < HEAD_DIM, o0, o1)
    for blk in range(tq // LANES):
        o_ref[blk * LANES:(blk + 1) * LANES, :] = ot[:, blk * LANES:(blk + 1) * LANES].T.astype(o_ref.dtype)


def fox_attention(fq, fk, fvt, c, tq=256):
    b, s, _ = fk.shape
    hp = N_FOX_HEADS // 2
    ct = c.reshape(b, hp, 2, s)
    cs = ct.transpose(0, 1, 3, 2)
    return pl.pallas_call(
        functools.partial(_fox_attn_kernel, tq=tq),
        out_shape=jax.ShapeDtypeStruct((b, s, N_FOX_HEADS * HEAD_DIM), BF16),
        grid=(b, hp, s // tq),
        in_specs=[pl.BlockSpec((None, tq, 2 * LANES), lambda bb, p, i: (bb, i, p)),
                  pl.BlockSpec((None, s, LANES), lambda bb, p, i: (bb, 0, p)),
                  pl.BlockSpec((LANES, s), lambda bb, p, i: (p, bb)),
                  pl.BlockSpec((None, None, 2, tq), lambda bb, p, i: (bb, p, 0, i)),
                  pl.BlockSpec((None, None, s, 2), lambda bb, p, i: (bb, p, 0, 0))],
        out_specs=pl.BlockSpec((None, tq, LANES), lambda bb, p, i: (bb, i, p)),
        scratch_shapes=[pltpu.VMEM((2 * tq, LANES), BF16), pltpu.VMEM((2, tq), F32), pltpu.VMEM((2, tq), F32),
                        pltpu.VMEM((2, LANES, tq), F32)],
        compiler_params=_cparams("parallel", "parallel", "arbitrary"),
        name="fox_attention",
    )(fq, fk, fvt, ct, cs)


FOX_QKV = 3 * N_FOX_HEADS * HEAD_DIM
NSA_Q = N_NSA_HEADS * HEAD_DIM
NSA_KV = 6 * N_NSA_KV * HEAD_DIM
NSA_GATES = 3 * N_NSA_HEADS
_QK_SCALE = HEAD_DIM ** -0.5

_EVEN_SEGS = ((0, 1024, _QK_SCALE), (1024, 512, 1.0), (1536, 1024, _QK_SCALE),
              (2560, 256, 1.0), (2816, 256, 1.0), (3072, 128, 1.0))
_EVEN_DTYPES = (BF16, BF16, BF16, F32, BF16, F32)
_EVEN_TSEGS = ((0, 512), (512, 128), (640, 128))


def _even_w_ext(w_in):
    d = w_in.shape[0]
    o = 0
    w_fq = w_in[:, o:o + 512].reshape(d, N_FOX_HEADS, HEAD_DIM); o += 512
    w_fk = w_in[:, o:o + 512]; o += 512
    w_fv = w_in[:, o:o + 512]; o += 512
    w_ff = w_in[:, o:o + N_FOX_HEADS]; o += N_FOX_HEADS
    w_nq = w_in[:, o:o + NSA_Q].reshape(d, N_NSA_HEADS, HEAD_DIM); o += NSA_Q
    w_nkv = w_in[:, o:o + NSA_KV]; o += NSA_KV
    w_ng = w_in[:, o:o + NSA_GATES]
    half_f = (np.arange(N_FOX_HEADS) % 2)[None, :, None, None] == np.arange(2)[None, None, :, None]
    fq = jnp.where(half_f, w_fq[:, :, None, :], 0.0).reshape(d, N_FOX_HEADS * LANES)
    half_n = (np.arange(N_NSA_HEADS) // NSA_GROUP)[None, :, None, None] == np.arange(2)[None, None, :, None]
    nq = jnp.where(half_n, w_nq[:, :, None, :], 0.0).reshape(d, N_NSA_HEADS * LANES)
    small = jnp.concatenate([w_ff, w_ng, jnp.zeros((d, LANES - N_FOX_HEADS - NSA_GATES), w_in.dtype)], axis=1)
    w = jnp.concatenate([fq, w_fk, nq, w_nkv[:, 0:256], w_nkv[:, 256:384], w_nkv[:, 512:640], small], axis=1)
    wt = jnp.concatenate([w_fv, w_nkv[:, 384:512], w_nkv[:, 640:768]], axis=1).T
    return w.astype(BF16), wt.astype(BF16)


def _nsa_compress_kernel(x_ref, pos_ref, w_ref, kc_ref, vct_ref):
    n = x_ref.shape[2]
    for kind in range(2):
        acc = jnp.zeros((n, LANES), F32)
        for g in range(N_NSA_KV):
            x = x_ref[kind, g]
            top = _dot((x + pos_ref[0:1, :]).astype(BF16), w_ref[kind, g, 0])
            bot = _dot((x + pos_ref[1:2, :]).astype(BF16), w_ref[kind, g, 1])
            acc = acc + top + pltpu.roll(bot, n - 1, 0)
        if kind == 0:
            kc_ref[...] = acc.astype(kc_ref.dtype)
        else:
            for blk in range(n // LANES):
                sl = slice(blk * LANES, (blk + 1) * LANES)
                vct_ref[:, sl] = acc[sl, :].T.astype(vct_ref.dtype)


def nsa_compress(ncmp, cmp_pos, w_cmp_k, w_cmp_v):
    b, s, _ = ncmp.shape
    n = s // CMP_STRIDE
    half = CMP_STRIDE * HEAD_DIM
    x = ncmp.reshape(b, n, CMP_STRIDE, 2, N_NSA_KV, HEAD_DIM).transpose(0, 3, 4, 1, 2, 5).reshape(b, 2, N_NSA_KV, n, half)
    pos = cmp_pos.astype(F32).reshape(2, half)
    w = jnp.stack([w_cmp_k, w_cmp_v]).astype(F32).reshape(2, 2, half, HEAD_DIM)
    place = (np.arange(N_NSA_KV)[:, None, None] == np.arange(2)[None, :, None])
    w_pad = jnp.where(place[None, :, None, None], w[:, None, :, :, None, :], 0.0)
    w_pad = w_pad.reshape(2, N_NSA_KV, 2, half, LANES).astype(BF16)
    return pl.pallas_call(
        _nsa_compress_kernel,
        out_shape=[jax.ShapeDtypeStruct((b, n, LANES), BF16), jax.ShapeDtypeStruct((b, LANES, n), BF16)],
        grid=(b,),
        in_specs=[pl.BlockSpec((None, 2, N_NSA_KV, n, half), lambda i: (i, 0, 0, 0, 0)),
                  pl.BlockSpec((2, half), lambda i: (0, 0)),
                  pl.BlockSpec((2, N_NSA_KV, 2, half, LANES), lambda i: (0, 0, 0, 0, 0))],
        out_specs=[pl.BlockSpec((None, n, LANES), lambda i: (i, 0, 0)),
                   pl.BlockSpec((None, LANES, n), lambda i: (i, 0, 0))],
        compiler_params=_cparams("parallel"),
        name="nsa_compress",
    )(x, pos, w_pad)


def _nsa_attn_kernel(q_ref, kc_ref, vct_ref, nk_ref, svt_ref, wvt_ref, gate_ref, slope_ref, ovt_ref, o_ref,
                     q4_ref, m_ref, l_ref, acc_ref, selt_ref, *, tq):
    g = pl.program_id(1)
    i = pl.program_id(2)
    t0 = i * tq
    n_cmp = kc_ref.shape[0]
    R = NSA_GROUP

    for r in range(R):
        q4_ref[r * tq:(r + 1) * tq, :] = q_ref[:, r * LANES:(r + 1) * LANES]
    q4 = q4_ref[...]

    t_c = t0 + lax.broadcasted_iota(jnp.int32, (n_cmp, tq), 1)
    cend = CMP_STRIDE * lax.broadcasted_iota(jnp.int32, (n_cmp, tq), 0) + (CMP_LEN - 1)
    cmask = cend <= t_c
    cdist = (t_c - cend).astype(F32)
    any_c = (t_c[0:1, :] >= CMP_LEN - 1).astype(F32)
    s_c = _dot_nt(kc_ref[...], q4)
    imp_t = jnp.zeros((LANES, tq), F32)
    o_cmp = []
    for r in range(R):
        s = s_c[:, r * tq:(r + 1) * tq] - slope_ref[r:r + 1, 0:1] * cdist
        s = jnp.where(cmask, s, NEG)
        e = jnp.exp(s - jnp.max(s, axis=0, keepdims=True))
        p = (e / jnp.sum(e, axis=0, keepdims=True)) * any_c
        pb = p.astype(BF16)
        o_cmp.append(_dot(vct_ref[...], pb))
        imp_t = imp_t + _dot(ovt_ref[...], pb)

    jrow = lax.broadcasted_iota(jnp.int32, (LANES, tq), 0)
    cur = (t0 + lax.broadcasted_iota(jnp.int32, (LANES, tq), 1)) // SLC_LEN
    forced = ((jrow == 0) | (jrow == cur) | (jrow == cur - 1)).astype(F32)
    imp_t = jnp.where(jrow <= cur, imp_t + FORCE_BONUS * forced, -jnp.inf)
    n_sel = svt_ref.shape[1] // SLC_LEN
    rank = jnp.zeros((LANES, tq), jnp.int32)
    for jp in range(n_sel):
        other = imp_t[jp:jp + 1, :]
        ahead = (other > imp_t) | ((other == imp_t) & (jp < jrow))
        rank = rank + ahead.astype(jnp.int32)
    selt_ref[...] = (rank < SLC_TOPK).astype(F32)

    KB = ATTN_KB
    t_pos = t0 + lax.broadcasted_iota(jnp.int32, (KB, tq), 1)
    s_in = lax.broadcasted_iota(jnp.int32, (KB, tq), 0)
    per_kb = KB // SLC_LEN
    m_ref[...] = jnp.full(m_ref.shape, -jnp.inf, F32)
    l_ref[...] = jnp.zeros(l_ref.shape, F32)
    acc_ref[...] = jnp.zeros(acc_ref.shape, F32)

    def sel_body(kb, carry):
        start = pl.multiple_of(kb * KB, KB)
        dist_i = t_pos - (start + s_in)
        dist = dist_i.astype(F32)
        chosen = jnp.concatenate(
            [jnp.broadcast_to(selt_ref[pl.ds(kb * per_kb + c, 1), :], (SLC_LEN, tq)) for c in range(per_kb)], axis=0)
        mask = (dist_i >= 0) & (chosen > 0.5)
        st = _dot_nt(nk_ref[pl.ds(start, KB), 0:LANES], q4)
        vt = svt_ref[:, pl.ds(start, KB)]
        for r in range(R):
            s = st[:, r * tq:(r + 1) * tq] - slope_ref[r:r + 1, 0:1] * dist
            s = jnp.where(mask, s, NEG)
            m_old = m_ref[r:r + 1, :]
            m_new = jnp.maximum(m_old, jnp.max(s, axis=0, keepdims=True))
            alpha = jnp.exp(m_old - m_new)
            p = jnp.exp(s - m_new)
            l_ref[r:r + 1, :] = alpha * l_ref[r:r + 1, :] + jnp.sum(p, axis=0, keepdims=True)
            acc_ref[r] = alpha * acc_ref[r] + _dot(vt, p.astype(BF16))
            m_ref[r:r + 1, :] = m_new
        return carry

    lax.fori_loop(0, t0 // KB + 1, sel_body, 0)

    nwb = (WINDOW + tq) // tq
    w_first = i - (nwb - 1)
    starts = [pl.multiple_of(jnp.maximum(w_first + d, 0) * tq, tq) for d in range(nwb)]
    kw = jnp.concatenate([nk_ref[pl.ds(st_, tq), LANES:2 * LANES] for st_ in starts], axis=0)
    vtw = jnp.concatenate([wvt_ref[:, pl.ds(st_, tq)] for st_ in starts], axis=1)
    wl = nwb * tq
    key_pos = w_first * tq + lax.broadcasted_iota(jnp.int32, (wl, tq), 0)
    wdist_i = (t0 + lax.broadcasted_iota(jnp.int32, (wl, tq), 1)) - key_pos
    wmask = (wdist_i >= 0) & (wdist_i < WINDOW) & (key_pos >= 0)
    wdist = wdist_i.astype(F32)
    stw = _dot_nt(kw, q4)
    o_win = []
    for r in range(R):
        s = jnp.where(wmask, stw[:, r * tq:(r + 1) * tq] - slope_ref[r:r + 1, 0:1] * wdist, NEG)
        e = jnp.exp(s - jnp.max(s, axis=0, keepdims=True))
        p = e / jnp.sum(e, axis=0, keepdims=True)
        o_win.append(_dot(vtw, p.astype(BF16)))

    gates = _sigmoid(gate_ref[...])
    row = lax.broadcasted_iota(jnp.int32, (LANES, tq), 0)
    mine = (row // HEAD_DIM) == g
    for r in range(R):
        ot = (gates[3 * r:3 * r + 1, :] * o_cmp[r] + gates[3 * r + 1:3 * r + 2, :] * (acc_ref[r] / l_ref[r:r + 1, :])
              + gates[3 * r + 2:3 * r + 3, :] * o_win[r])
        ot = jnp.where(mine, ot, 0.0)
        for blk in range(tq // LANES):
            o_ref[blk * LANES:(blk + 1) * LANES, r * LANES:(r + 1) * LANES] = (
                ot[:, blk * LANES:(blk + 1) * LANES].T.astype(o_ref.dtype))


def _alibi_slopes(n):
    return np.asarray(2.0 ** (-8.0 * np.arange(1, n + 1) / n), np.float32)


def nsa_attention(nq, kc, vct, nk, svt, wvt, gates_t, tq=128):
    b, s, _ = nk.shape
    n_cmp = s // CMP_STRIDE
    n_sel = s // SLC_LEN
    slopes = np.broadcast_to(_alibi_slopes(N_NSA_HEADS).reshape(N_NSA_KV, NSA_GROUP, 1), (N_NSA_KV, NSA_GROUP, LANES))
    slopes = jnp.asarray(np.concatenate([slopes, np.zeros((N_NSA_KV, SUBLANES - NSA_GROUP, LANES), np.float32)], axis=1))
    cstart = CMP_STRIDE * np.arange(n_cmp)
    sstart = SLC_LEN * np.arange(LANES)
    ov = ((cstart[None, :] < sstart[:, None] + SLC_LEN) & (cstart[None, :] + CMP_LEN > sstart[:, None])
          & (np.arange(LANES)[:, None] < n_sel) & (cstart[None, :] + CMP_LEN <= s))
    ovt = jnp.asarray(ov, BF16)
    return pl.pallas_call(
        functools.partial(_nsa_attn_kernel, tq=tq),
        out_shape=jax.ShapeDtypeStruct((b, s, N_NSA_HEADS * LANES), BF16),
        grid=(b, N_NSA_KV, s // tq),
        in_specs=[pl.BlockSpec((None, tq, NSA_GROUP * LANES), lambda bb, g, i: (bb, i, g)),
                  pl.BlockSpec((None, n_cmp, LANES), lambda bb, g, i: (bb, 0, 0)),
                  pl.BlockSpec((None, LANES, n_cmp), lambda bb, g, i: (bb, 0, 0)),
                  pl.BlockSpec((None, s, 2 * LANES), lambda bb, g, i: (bb, 0, 0)),
                  pl.BlockSpec((LANES, s), lambda bb, g, i: (0, bb)),
                  pl.BlockSpec((LANES, s), lambda bb, g, i: (0, bb)),
                  pl.BlockSpec((None, None, 2 * SUBLANES, tq), lambda bb, g, i: (bb, g, 0, i)),
                  pl.BlockSpec((None, SUBLANES, LANES), lambda bb, g, i: (g, 0, 0)),
                  pl.BlockSpec((LANES, n_cmp), lambda bb, g, i: (0, 0))],
        out_specs=pl.BlockSpec((None, tq, NSA_GROUP * LANES), lambda bb, g, i: (bb, i, g)),
        scratch_shapes=[pltpu.VMEM((NSA_GROUP * tq, LANES), BF16),
                        pltpu.VMEM((NSA_GROUP, tq), F32), pltpu.VMEM((NSA_GROUP, tq), F32),
                        pltpu.VMEM((NSA_GROUP, LANES, tq), F32), pltpu.VMEM((LANES, tq), F32)],
        compiler_params=_cparams("parallel", "parallel", "arbitrary"),
        name="nsa_attention",
    )(nq, kc, vct, nk, svt, wvt, gates_t, slopes, ovt)


def _peer_cand_layout():
    K = PEER_TOPK
    groups = [((0, 1), (0, K))]
    groups += [((j1, j1 + 1), (0, SUBLANES)) for j1 in range(1, SUBLANES)]
    groups += [((SUBLANES, K), (0, 1))]
    flat = []
    for (a0, a1), (b0, b1) in groups:
        flat += [j1 * K + j2 for j1 in range(a0, a1) for j2 in range(b0, b1)]
    return groups, np.asarray(flat, np.int32)


def _extract_top(vals, tags, n, write):
    big = jnp.int32(2 ** 30)
    for it in range(n):
        m = jnp.max(vals, axis=0, keepdims=True)
        tag = jnp.min(jnp.where(vals == m, tags, big), axis=0, keepdims=True)
        write(it, m, tag)
        vals = jnp.where(tags == tag, -jnp.inf, vals)


def _peer_topk_kernel(q_ref, keys_ref, flat_ref, a_ref, b_ref, g_ref,
                      s1_ref, s2_ref, i1_ref, i2_ref, bs_ref, bid_ref, at_ref, bt_ref, gt_ref, *, tm, groups):
    K = PEER_TOPK
    kidx = lax.broadcasted_iota(jnp.int32, (PEER_NKEYS, tm), 0)
    flat = flat_ref[...]
    for h in range(PEER_HEADS):
        for c, (s_ref, i_ref) in enumerate(((s1_ref, i1_ref), (s2_ref, i2_ref))):
            blk = (2 * h + c) * LANES
            scores = _dot_nt(keys_ref[h, c], q_ref[:, blk:blk + LANES])

            def write(it, m, tag, s_ref=s_ref, i_ref=i_ref):
                s_ref[it:it + 1, :] = m
                i_ref[it:it + 1, :] = tag

            _extract_top(scores, kidx, K, write)
        s1, s2, i1, i2 = s1_ref[...], s2_ref[...], i1_ref[...], i2_ref[...]
        cand, ids = [], []
        for (a0, a1), (b0, b1) in groups:
            cand.append(s1[a0:a1] + s2[b0:b1])
            ids.append(i1[a0:a1] * PEER_NKEYS + i2[b0:b1])
        cand = jnp.concatenate(cand, axis=0)
        ids = jnp.concatenate(ids, axis=0)

        def write2(it, m, tag):
            bs_ref[it:it + 1, :] = m
            bid_ref[it:it + 1, :] = jnp.max(jnp.where(flat == tag, ids, -1), axis=0, keepdims=True)

        _extract_top(cand, flat, K, write2)
        bs = bs_ref[...]
        e = jnp.exp(bs - jnp.max(bs, axis=0, keepdims=True))
        gt_ref[h * K:(h + 1) * K, :] = e / jnp.sum(e, axis=0, keepdims=True)
        bid = bid_ref[...]
        at_ref[h * K:(h + 1) * K, :] = bid // PEER_NKEYS
        bt_ref[h * K:(h + 1) * K, :] = bid % PEER_NKEYS
    for blk in range(tm // LANES):
        sl = slice(blk * LANES, (blk + 1) * LANES)
        a_ref[sl, :] = at_ref[:, sl].T
        b_ref[sl, :] = bt_ref[:, sl].T
        g_ref[sl, :] = gt_ref[:, sl].T


def peer_topk(q_bf16, sub_keys, tm=256):
    n = q_bf16.shape[0]
    groups, flat = _peer_cand_layout()
    nc = flat.shape[0]
    flat_b = jnp.asarray(np.broadcast_to(flat[:, None], (nc, tm)))
    hk = PEER_HEADS * PEER_TOPK
    row = lambda dt: jax.ShapeDtypeStruct((n, hk), dt)
    return pl.pallas_call(
        functools.partial(_peer_topk_kernel, tm=tm, groups=groups),
        out_shape=[row(jnp.int32), row(jnp.int32), row(F32)],
        grid=(n // tm,),
        in_specs=[pl.BlockSpec((tm, q_bf16.shape[1]), lambda i: (i, 0)),
                  pl.BlockSpec(sub_keys.shape, lambda i: (0, 0, 0, 0)),
                  pl.BlockSpec((nc, tm), lambda i: (0, 0))],
        out_specs=[pl.BlockSpec((tm, hk), lambda i: (i, 0))] * 3,
        scratch_shapes=[pltpu.VMEM((PEER_TOPK, tm), F32), pltpu.VMEM((PEER_TOPK, tm), F32),
                        pltpu.VMEM((PEER_TOPK, tm), jnp.int32), pltpu.VMEM((PEER_TOPK, tm), jnp.int32),
                        pltpu.VMEM((PEER_TOPK, tm), F32), pltpu.VMEM((PEER_TOPK, tm), jnp.int32),
                        pltpu.VMEM((hk, tm), jnp.int32), pltpu.VMEM((hk, tm), jnp.int32),
                        pltpu.VMEM((hk, tm), F32)],
        compiler_params=_cparams("parallel"),
        name="peer_topk",
    )(q_bf16, sub_keys.astype(BF16), flat_b)


_W_PITCH = PEER_NKEYS + SUBLANES


def _peer_apply_kernel(x_ref, u_ref, v_ref, a_ref, b_ref, g_ref, r_ref, o_ref, w_ref, *, tm, ec, unroll):
    c = pl.program_id(1)
    slabs = ec // PEER_NKEYS

    @pl.when(c == 0)
    def _():
        o_ref[...] = r_ref[...]
        sub = lax.broadcasted_iota(jnp.int32, (PEER_NKEYS, LANES), 0)

        def build(tb, carry):
            for tt in range(unroll):
                t = tb * unroll + tt
                a = a_ref[pl.ds(t, 1), :]
                b = b_ref[pl.ds(t, 1), :]
                g = g_ref[pl.ds(t, 1), :]
                pt = jnp.where(sub == a, g, 0.0).astype(BF16)
                qt = jnp.where(sub == b, 1.0, 0.0).astype(BF16)
                w_ref[pl.ds(pl.multiple_of(t * _W_PITCH, SUBLANES), PEER_NKEYS), :] = _dot_nt(pt, qt)
            return carry

        lax.fori_loop(0, tm // unroll, build, 0)

    act = _dot_nt(x_ref[...], u_ref[...])
    w = jnp.concatenate([w_ref[pl.ds(c * slabs + s, tm, stride=_W_PITCH), :] for s in range(slabs)], axis=1)
    o_ref[...] += _dot((_gelu(act) * w).astype(BF16), v_ref[...])


def peer_apply(xn_bf16, u_bf16, v_bf16, a_idx, b_idx, gw, resid, tm=256, ec=2048, unroll=8):
    n, d = xn_bf16.shape
    ne = u_bf16.shape[0]
    tok = lambda w: pl.BlockSpec((tm, w), lambda i, c: (i, 0))
    return pl.pallas_call(
        functools.partial(_peer_apply_kernel, tm=tm, ec=ec, unroll=unroll),
        out_shape=jax.ShapeDtypeStruct((n, d), F32),
        grid=(n // tm, ne // ec),
        in_specs=[tok(d),
                  pl.BlockSpec((ec, d), lambda i, c: (c, 0)),
                  pl.BlockSpec((ec, d), lambda i, c: (c, 0)),
                  tok(a_idx.shape[1]), tok(b_idx.shape[1]), tok(gw.shape[1]), tok(d)],
        out_specs=tok(d),
        scratch_shapes=[pltpu.VMEM((tm * _W_PITCH, LANES), F32)],
        compiler_params=pltpu.CompilerParams(dimension_semantics=("parallel", "arbitrary"),
                                             vmem_limit_bytes=56 * 1024 * 1024),
        name="peer_apply",
    )(xn_bf16, u_bf16, v_bf16, a_idx, b_idx, gw, resid)


def _odd_mixer_kernel(h_ref, g_ref, win_ref, cw_ref, cb_ref, wra_ref, bra_ref, wri_ref, bri_ref, lam_ref, wout_ref,
                      o_ref, xbuf_ref, carry_ref, *, tc):
    j = pl.program_id(1)
    d_rnn = cw_ref.shape[1]
    pad = SUBLANES

    @pl.when(j == 0)
    def _():
        xbuf_ref[0:pad, :] = jnp.zeros((pad, d_rnn), F32)
        carry_ref[...] = jnp.zeros(carry_ref.shape, F32)

    h_in = h_ref[...]
    z = _dot(_rmsnorm_rows(h_in, g_ref[...]).astype(BF16), win_ref[...])
    gate = z[:, :d_rnn]
    xbuf_ref[pad:pad + tc, :] = z[:, d_rnn:]
    xc = cb_ref[...] + cw_ref[0:1, :] * xbuf_ref[pl.ds(pad - 3, tc), :]
    for w in range(1, CONV_W):
        xc = xc + cw_ref[w:w + 1, :] * xbuf_ref[pl.ds(pad - 3 + w, tc), :]
    xbuf_ref[0:pad, :] = xbuf_ref[tc:tc + pad, :]

    xcb = xc.astype(BF16)
    nb = d_rnn // RNN_BLOCK_W
    blk = lambda w_ref: jnp.concatenate(
        [_dot(xcb[:, n * RNN_BLOCK_W:(n + 1) * RNN_BLOCK_W], w_ref[n]) for n in range(nb)], axis=1)
    r = _sigmoid(blk(wra_ref) + bra_ref[...])
    gi = _sigmoid(blk(wri_ref) + bri_ref[...])
    nl = -lam_ref[...]
    softplus = jnp.maximum(nl, 0.0) + jnp.log(1.0 + jnp.exp(-jnp.abs(nl)))
    log_a = (-LRU_C) * r * softplus
    a = jnp.exp(log_a)
    u = jnp.sqrt(1.0 - jnp.exp(2.0 * log_a)) * (gi * xc)

    row = lax.broadcasted_iota(jnp.int32, (tc, d_rnn), 0)
    shift = 1
    while shift < tc:
        valid = row >= shift
        a_prev = jnp.where(valid, pltpu.roll(a, shift, 0), 1.0)
        u_prev = jnp.where(valid, pltpu.roll(u, shift, 0), 0.0)
        u = u + a * u_prev
        a = a * a_prev
        shift *= 2
    hs = u + a * carry_ref[...]
    carry_ref[...] = hs[tc - 1:tc, :]

    y = (_gelu(gate) * hs).astype(BF16)
    o_ref[...] = h_in + _dot(y, wout_ref[...])


def odd_mixer_residual(h, norm_g, w_in, conv_w, conv_b, w_ra, b_ra, w_ri, b_ri, lam, w_out, tc=256):
    b, s, d = h.shape
    d_rnn = conv_w.shape[1]
    full = lambda a: pl.BlockSpec(a.shape, lambda bb, j, nd=a.ndim: (0,) * nd)
    row = lambda v: v.astype(F32).reshape(1, -1)
    args = [norm_g.astype(F32).reshape(1, d), w_in.astype(BF16), conv_w.astype(F32), row(conv_b),
            w_ra.astype(BF16), row(b_ra), w_ri.astype(BF16), row(b_ri), row(lam), w_out.astype(BF16)]
    return pl.pallas_call(
        functools.partial(_odd_mixer_kernel, tc=tc),
        out_shape=jax.ShapeDtypeStruct((b, s, d), F32),
        grid=(b, s // tc),
        in_specs=[pl.BlockSpec((None, tc, d), lambda bb, j: (bb, j, 0))] + [full(a) for a in args],
        out_specs=pl.BlockSpec((None, tc, d), lambda bb, j: (bb, j, 0)),
        scratch_shapes=[pltpu.VMEM((tc + 2 * SUBLANES, d_rnn), F32), pltpu.VMEM((1, d_rnn), F32)],
        compiler_params=_cparams("parallel", "arbitrary"),
        name="odd_mixer",
    )(h, *args)


def _rmsnorm_kernel(x_ref, g_ref, o_ref):
    o_ref[...] = _rmsnorm_rows(x_ref[...], g_ref[...])


def rmsnorm_rows(x2d, g, tm=512):
    n, d = x2d.shape
    return pl.pallas_call(
        _rmsnorm_kernel,
        out_shape=jax.ShapeDtypeStruct((n, d), F32),
        grid=(n // tm,),
        in_specs=[pl.BlockSpec((tm, d), lambda i: (i, 0)), pl.BlockSpec((1, d), lambda i: (0, 0))],
        out_specs=pl.BlockSpec((tm, d), lambda i: (i, 0)),
        compiler_params=_cparams("parallel"),
        name="final_rmsnorm",
    )(x2d, g.astype(F32).reshape(1, d))


def even_mixer_residual(h, norm_g, w_in, b_f, cmp_pos, w_cmp_k, w_cmp_v, w_out):
    b, s, d = h.shape
    h2d = h.reshape(b * s, d)
    w_ext, wt_ext = _even_w_ext(w_in)
    outs = norm_proj(h2d, norm_g.astype(F32), w_ext, _EVEN_SEGS, _EVEN_DTYPES, wt_ext, _EVEN_TSEGS, name="even_in_proj")
    fq, fk, nq, ncmp, nk, small = [o.reshape(b, s, -1) for o in outs[:6]]
    fvt, svt, wvt = outs[6:]
    c = fox_gate_cumsum(small[..., :N_FOX_HEADS].transpose(0, 2, 1), b_f)
    o_fox = fox_attention(fq, fk, fvt, c)
    kc, vct = nsa_compress(ncmp, cmp_pos, w_cmp_k, w_cmp_v)
    gates_t = small[..., N_FOX_HEADS:N_FOX_HEADS + NSA_GATES].reshape(b, s, N_NSA_KV, 3 * NSA_GROUP).transpose(0, 2, 3, 1)
    gates_t = jnp.pad(gates_t, ((0, 0), (0, 0), (0, 2 * SUBLANES - 3 * NSA_GROUP), (0, 0)))
    o_nsa = nsa_attention(nq, kc, vct, nk, svt, wvt, gates_t)
    nf = N_FOX_HEADS * HEAD_DIM
    w_n = w_out[nf:].reshape(N_NSA_HEADS, HEAD_DIM, d)
    half_n = (np.arange(N_NSA_HEADS) // NSA_GROUP)[:, None, None, None] == np.arange(2)[None, :, None, None]
    w_n = jnp.where(half_n, w_n[:, None], 0.0).reshape(N_NSA_HEADS * LANES, d)
    pairs = [(o_fox.reshape(b * s, -1), w_out[:nf].astype(BF16)), (o_nsa.reshape(b * s, -1), w_n.astype(BF16))]
    return proj_residual(pairs, h2d, name="even_out_proj").reshape(b, s, d)


def peer_ffn_residual(h2d, norm_g, w_q, sub_keys, u_tab, v_tab):
    d = h2d.shape[1]
    nq = w_q.shape[1]
    q, xn = norm_proj(h2d, norm_g, w_q.astype(BF16), ((0, nq, 1.0), (None, d, 1.0)), (BF16, BF16), name="peer_query")
    a_idx, b_idx, gw = peer_topk(q, sub_keys)
    return peer_apply(xn, u_tab.astype(BF16), v_tab.astype(BF16), a_idx, b_idx, gw, h2d)


def kernel(x, even_norm_g, even_w_in, even_b_f, even_cmp_pos, even_w_cmp_k, even_w_cmp_v, even_w_out, odd_norm_g, odd_w_in, odd_conv_w, odd_conv_b, odd_w_ra, odd_b_ra, odd_w_ri, odd_b_ri, odd_lam, odd_w_out, ffn_norm_g, peer_w_q, peer_sub_keys, peer_u, peer_v, final_g):
    b, s, d = x.shape
    depth = ffn_norm_g.shape[0]
    h = x
    for layer in range(depth):
        j = layer // 2
        if layer % 2 == 0:
            h = even_mixer_residual(h, even_norm_g[j], even_w_in[j], even_b_f[j], even_cmp_pos[j],
                                    even_w_cmp_k[j], even_w_cmp_v[j], even_w_out[j])
        else:
            h = odd_mixer_residual(h, odd_norm_g[j], odd_w_in[j], odd_conv_w[j], odd_conv_b[j], odd_w_ra[j],
                                   odd_b_ra[j], odd_w_ri[j], odd_b_ri[j], odd_lam[j], odd_w_out[j])
        h = peer_ffn_residual(h.reshape(b * s, d), ffn_norm_g[layer].astype(F32), peer_w_q[layer],
                              peer_sub_keys[layer], peer_u[layer], peer_v[layer]).reshape(b, s, d)
    return rmsnorm_rows(h.reshape(b * s, d), final_g).reshape(b, s, d)
```

```python
import functools
import math

import jax
import jax.numpy as jnp
import numpy as np
from jax import lax
from jax.experimental import pallas as pl
from jax.experimental.pallas import tpu as pltpu

HEAD_DIM = 64
N_FOX_HEADS = 8
N_NSA_HEADS = 8
N_NSA_KV = 2
NSA_GROUP = N_NSA_HEADS // N_NSA_KV
CMP_LEN = 32
CMP_STRIDE = 16
SLC_LEN = 64
SLC_TOPK = 16
WINDOW = 512
FORCE_BONUS = 1.0e4
NEG = -1.0e30
D_RNN = 1280
RNN_BLOCKS = 10
RNN_BLOCK_W = D_RNN // RNN_BLOCKS
CONV_W = 4
LRU_C = 8.0
PEER_HEADS = 8
PEER_NKEYS = 128
PEER_HALF = 128
PEER_TOPK = 16
RMS_EPS = 1e-6

LANES = 128
SUBLANES = 8
VMEM_LIMIT = 48 * 1024 * 1024

F32 = jnp.float32
BF16 = jnp.bfloat16


def _cparams(*sem):
    return pltpu.CompilerParams(dimension_semantics=sem, vmem_limit_bytes=VMEM_LIMIT)


def _dot(a, b):
    return lax.dot_general(a, b, (((1,), (0,)), ((), ())), preferred_element_type=F32)


def _dot_nt(a, b):
    return lax.dot_general(a, b, (((1,), (1,)), ((), ())), preferred_element_type=F32)


def _rmsnorm_rows(x, g):
    return x * lax.rsqrt(jnp.mean(x * x, axis=-1, keepdims=True) + RMS_EPS) * g


def _gelu(x):
    return 0.5 * x * (1.0 + lax.erf(x * (1.0 / math.sqrt(2.0))))


def _sigmoid(x):
    return 1.0 / (1.0 + jnp.exp(-x))


def _norm_proj_kernel(x_ref, g_ref, w_ref, wt_ref, *out_refs, segs, tsegs):
    xn = _rmsnorm_rows(x_ref[...], g_ref[...]).astype(BF16)
    for o_ref, (start, width, scale) in zip(out_refs, segs):
        if start is None:
            o_ref[...] = xn
            continue
        y = _dot(xn, w_ref[:, start:start + width])
        if scale != 1.0:
            y = y * scale
        o_ref[...] = y.astype(o_ref.dtype)
    for o_ref, (start, width) in zip(out_refs[len(segs):], tsegs):
        o_ref[...] = _dot_nt(wt_ref[start:start + width, :], xn).astype(o_ref.dtype)


def norm_proj(x2d, g, w_bf16, segs, dtypes, wt_bf16=None, tsegs=(), tm=256, name="norm_proj"):
    n, d = x2d.shape
    if wt_bf16 is None:
        wt_bf16 = jnp.zeros((SUBLANES, d), BF16)
    outs = [jax.ShapeDtypeStruct((n, wd), dt) for (_, wd, _), dt in zip(segs, dtypes)]
    outs += [jax.ShapeDtypeStruct((wd, n), BF16) for (_, wd) in tsegs]
    return pl.pallas_call(
        functools.partial(_norm_proj_kernel, segs=tuple(segs), tsegs=tuple(tsegs)),
        out_shape=outs,
        grid=(n // tm,),
        in_specs=[pl.BlockSpec((tm, d), lambda i: (i, 0)),
                  pl.BlockSpec((1, d), lambda i: (0, 0)),
                  pl.BlockSpec(w_bf16.shape, lambda i: (0, 0)),
                  pl.BlockSpec(wt_bf16.shape, lambda i: (0, 0))],
        out_specs=([pl.BlockSpec((tm, wd), lambda i: (i, 0)) for (_, wd, _) in segs]
                   + [pl.BlockSpec((wd, tm), lambda i: (0, i)) for (_, wd) in tsegs]),
        compiler_params=_cparams("parallel"),
        name=name,
    )(x2d, g.reshape(1, d), w_bf16, wt_bf16)


def _proj_residual_kernel(*refs):
    r_ref, o_ref = refs[-2], refs[-1]
    acc = r_ref[...]
    for a_ref, w_ref in zip(refs[0:-2:2], refs[1:-2:2]):
        acc = acc + _dot(a_ref[...], w_ref[...])
    o_ref[...] = acc


def proj_residual(pairs, resid, tm=512, name="proj_residual"):
    n, d = resid.shape
    args, specs = [], []
    for a, w in pairs:
        args += [a, w]
        specs += [pl.BlockSpec((tm, a.shape[1]), lambda i: (i, 0)), pl.BlockSpec(w.shape, lambda i: (0, 0))]
    return pl.pallas_call(
        _proj_residual_kernel,
        out_shape=jax.ShapeDtypeStruct((n, d), F32),
        grid=(n // tm,),
        in_specs=specs + [pl.BlockSpec((tm, d), lambda i: (i, 0))],
        out_specs=pl.BlockSpec((tm, d), lambda i: (i, 0)),
        compiler_params=_cparams("parallel"),
        name=name,
    )(*args, resid)


def _fox_gate_kernel(f_ref, b_ref, c_ref):
    z = f_ref[...] + b_ref[...]
    x = jnp.minimum(z, 0.0) - jnp.log(1.0 + jnp.exp(-jnp.abs(z)))
    s = x.shape[-1]
    lane = lax.broadcasted_iota(jnp.int32, x.shape, 1)
    shift = 1
    while shift < s:
        x = x + jnp.where(lane >= shift, pltpu.roll(x, shift, 1), 0.0)
        shift *= 2
    c_ref[...] = x


def fox_gate_cumsum(f_t, b_f):
    b, h, s = f_t.shape
    return pl.pallas_call(
        _fox_gate_kernel,
        out_shape=jax.ShapeDtypeStruct((b, h, s), F32),
        grid=(b,),
        in_specs=[pl.BlockSpec((None, h, s), lambda i: (i, 0, 0)),
                  pl.BlockSpec((h, 1), lambda i: (0, 0))],
        out_specs=pl.BlockSpec((None, h, s), lambda i: (i, 0, 0)),
        compiler_params=_cparams("parallel"),
        name="fox_gate_cumsum",
    )(f_t, b_f.reshape(h, 1).astype(F32))


ATTN_KB = 512


def _fox_attn_kernel(q_ref, k_ref, vt_ref, ct_ref, cs_ref, o_ref, qs_ref, m_ref, l_ref, acc_ref, *, tq):
    KB = ATTN_KB
    i = pl.program_id(2)
    t0 = i * tq
    for hh in range(2):
        qs_ref[hh * tq:(hh + 1) * tq, :] = q_ref[:, hh * LANES:(hh + 1) * LANES]
    qs = qs_ref[...]
    m_ref[...] = jnp.full(m_ref.shape, -jnp.inf, F32)
    l_ref[...] = jnp.zeros(l_ref.shape, F32)
    acc_ref[...] = jnp.zeros(acc_ref.shape, F32)
    t_pos = t0 + lax.broadcasted_iota(jnp.int32, (KB, tq), 1)
    s_in = lax.broadcasted_iota(jnp.int32, (KB, tq), 0)

    def step(kb, masked):
        start = pl.multiple_of(kb * KB, KB)
        st = _dot_nt(k_ref[pl.ds(start, KB), :], qs)
        vt = vt_ref[:, pl.ds(start, KB)]
        cs = cs_ref[pl.ds(start, KB), :]
        for hh in range(2):
            s = st[:, hh * tq:(hh + 1) * tq] + (ct_ref[hh:hh + 1, :] - cs[:, hh:hh + 1])
            if masked:
                s = jnp.where(start + s_in <= t_pos, s, NEG)
            m_old = m_ref[hh:hh + 1, :]
            m_new = jnp.maximum(m_old, jnp.max(s, axis=0, keepdims=True))
            alpha = jnp.exp(m_old - m_new)
            p = jnp.exp(s - m_new)
            l_ref[hh:hh + 1, :] = alpha * l_ref[hh:hh + 1, :] + jnp.sum(p, axis=0, keepdims=True)
            acc_ref[hh] = alpha * acc_ref[hh] + _dot(vt, p.astype(BF16))
            m_ref[hh:hh + 1, :] = m_new

    n_full = t0 // KB

    def full_body(kb, carry):
        step(kb, False)
        return carry

    lax.fori_loop(0, n_full, full_body, 0)
    step(n_full, True)

    row = lax.broadcasted_iota(jnp.int32, (LANES, tq), 0)
    o0 = acc_ref[0] / l_ref[0:1, :]
    o1 = acc_ref[1] / l_ref[1:2, :]
    ot = jnp.where(row < HEAD_DIM, o0, o1)
    for blk in range(tq // LANES):
        o_ref[blk * LANES:(blk + 1) * LANES, :] = ot[:, blk * LANES:(blk + 1) * LANES].T.astype(o_ref.dtype)


def fox_attention(fq, fk, fvt, c, tq=256):
    b, s, _ = fk.shape
    hp = N_FOX_HEADS // 2
    ct = c.reshape(b, hp, 2, s)
    cs = ct.transpose(0, 1, 3, 2)
    return pl.pallas_call(
        functools.partial(_fox_attn_kernel, tq=tq),
        out_shape=jax.ShapeDtypeStruct((b, s, N_FOX_HEADS * HEAD_DIM), BF16),
        grid=(b, hp, s // tq),
        in_specs=[pl.BlockSpec((None, tq, 2 * LANES), lambda bb, p, i: (bb, i, p)),
                  pl.BlockSpec((None, s, LANES), lambda bb, p, i: (bb, 0, p)),
                  pl.BlockSpec((LANES, s), lambda bb, p, i: (p, bb)),
                  pl.BlockSpec((None, None, 2, tq), lambda bb, p, i: (bb, p, 0, i)),
                  pl.BlockSpec((None, None, s, 2), lambda bb, p, i: (bb, p, 0, 0))],
        out_specs=pl.BlockSpec((None, tq, LANES), lambda bb, p, i: (bb, i, p)),
        scratch_shapes=[pltpu.VMEM((2 * tq, LANES), BF16), pltpu.VMEM((2, tq), F32), pltpu.VMEM((2, tq), F32),
                        pltpu.VMEM((2, LANES, tq), F32)],
        compiler_params=_cparams("parallel", "parallel", "arbitrary"),
        name="fox_attention",
    )(fq, fk, fvt, ct, cs)


FOX_QKV = 3 * N_FOX_HEADS * HEAD_DIM
NSA_Q = N_NSA_HEADS * HEAD_DIM
NSA_KV = 6 * N_NSA_KV * HEAD_DIM
NSA_GATES = 3 * N_NSA_HEADS
_QK_SCALE = HEAD_DIM ** -0.5

_EVEN_SEGS = ((0, 1024, _QK_SCALE), (1024, 512, 1.0), (1536, 1024, _QK_SCALE),
              (2560, 256, 1.0), (2816, 256, 1.0), (3072, 128, 1.0))
_EVEN_DTYPES = (BF16, BF16, BF16, F32, BF16, F32)
_EVEN_TSEGS = ((0, 512), (512, 128), (640, 128))


def _even_w_ext(w_in):
    d = w_in.shape[0]
    o = 0
    w_fq = w_in[:, o:o + 512].reshape(d, N_FOX_HEADS, HEAD_DIM); o += 512
    w_fk = w_in[:, o:o + 512]; o += 512
    w_fv = w_in[:, o:o + 512]; o += 512
    w_ff = w_in[:, o:o + N_FOX_HEADS]; o += N_FOX_HEADS
    w_nq = w_in[:, o:o + NSA_Q].reshape(d, N_NSA_HEADS, HEAD_DIM); o += NSA_Q
    w_nkv = w_in[:, o:o + NSA_KV]; o += NSA_KV
    w_ng = w_in[:, o:o + NSA_GATES]
    half_f = (np.arange(N_FOX_HEADS) % 2)[None, :, None, None] == np.arange(2)[None, None, :, None]
    fq = jnp.where(half_f, w_fq[:, :, None, :], 0.0).reshape(d, N_FOX_HEADS * LANES)
    half_n = (np.arange(N_NSA_HEADS) // NSA_GROUP)[None, :, None, None] == np.arange(2)[None, None, :, None]
    nq = jnp.where(half_n, w_nq[:, :, None, :], 0.0).reshape(d, N_NSA_HEADS * LANES)
    small = jnp.concatenate([w_ff, w_ng, jnp.zeros((d, LANES - N_FOX_HEADS - NSA_GATES), w_in.dtype)], axis=1)
    w = jnp.concatenate([fq, w_fk, nq, w_nkv[:, 0:256], w_nkv[:, 256:384], w_nkv[:, 512:640], small], axis=1)
    wt = jnp.concatenate([w_fv, w_nkv[:, 384:512], w_nkv[:, 640:768]], axis=1).T
    return w.astype(BF16), wt.astype(BF16)


def _nsa_compress_kernel(x_ref, pos_ref, w_ref, kc_ref, vct_ref):
    n = x_ref.shape[2]
    for kind in range(2):
        acc = jnp.zeros((n, LANES), F32)
        for g in range(N_NSA_KV):
            x = x_ref[kind, g]
            top = _dot((x + pos_ref[0:1, :]).astype(BF16), w_ref[kind, g, 0])
            bot = _dot((x + pos_ref[1:2, :]).astype(BF16), w_ref[kind, g, 1])
            acc = acc + top + pltpu.roll(bot, n - 1, 0)
        if kind == 0:
            kc_ref[...] = acc.astype(kc_ref.dtype)
        else:
            for blk in range(n // LANES):
                sl = slice(blk * LANES, (blk + 1) * LANES)
                vct_ref[:, sl] = acc[sl, :].T.astype(vct_ref.dtype)


def nsa_compress(ncmp, cmp_pos, w_cmp_k, w_cmp_v):
    b, s, _ = ncmp.shape
    n = s // CMP_STRIDE
    half = CMP_STRIDE * HEAD_DIM
    x = ncmp.reshape(b, n, CMP_STRIDE, 2, N_NSA_KV, HEAD_DIM).transpose(0, 3, 4, 1, 2, 5).reshape(b, 2, N_NSA_KV, n, half)
    pos = cmp_pos.astype(F32).reshape(2, half)
    w = jnp.stack([w_cmp_k, w_cmp_v]).astype(F32).reshape(2, 2, half, HEAD_DIM)
    place = (np.arange(N_NSA_KV)[:, None, None] == np.arange(2)[None, :, None])
    w_pad = jnp.where(place[None, :, None, None], w[:, None, :, :, None, :], 0.0)
    w_pad = w_pad.reshape(2, N_NSA_KV, 2, half, LANES).astype(BF16)
    return pl.pallas_call(
        _nsa_compress_kernel,
        out_shape=[jax.ShapeDtypeStruct((b, n, LANES), BF16), jax.ShapeDtypeStruct((b, LANES, n), BF16)],
        grid=(b,),
        in_specs=[pl.BlockSpec((None, 2, N_NSA_KV, n, half), lambda i: (i, 0, 0, 0, 0)),
                  pl.BlockSpec((2, half), lambda i: (0, 0)),
                  pl.BlockSpec((2, N_NSA_KV, 2, half, LANES), lambda i: (0, 0, 0, 0, 0))],
        out_specs=[pl.BlockSpec((None, n, LANES), lambda i: (i, 0, 0)),
                   pl.BlockSpec((None, LANES, n), lambda i: (i, 0, 0))],
        compiler_params=_cparams("parallel"),
        name="nsa_compress",
    )(x, pos, w_pad)


def _nsa_attn_kernel(q_ref, kc_ref, vct_ref, nk_ref, svt_ref, wvt_ref, gate_ref, slope_ref, ovt_ref, o_ref,
                     q4_ref, m_ref, l_ref, acc_ref, selt_ref, *, tq):
    g = pl.program_id(1)
    i = pl.program_id(2)
    t0 = i * tq
    n_cmp = kc_ref.shape[0]
    R = NSA_GROUP

    for r in range(R):
        q4_ref[r * tq:(r + 1) * tq, :] = q_ref[:, r * LANES:(r + 1) * LANES]
    q4 = q4_ref[...]

    t_c = t0 + lax.broadcasted_iota(jnp.int32, (n_cmp, tq), 1)
    cend = CMP_STRIDE * lax.broadcasted_iota(jnp.int32, (n_cmp, tq), 0) + (CMP_LEN - 1)
    cmask = cend <= t_c
    cdist = (t_c - cend).astype(F32)
    any_c = (t_c[0:1, :] >= CMP_LEN - 1).astype(F32)
    s_c = _dot_nt(kc_ref[...], q4)
    imp_t = jnp.zeros((LANES, tq), F32)
    o_cmp = []
    for r in range(R):
        s = s_c[:, r * tq:(r + 1) * tq] - slope_ref[r:r + 1, 0:1] * cdist
        s = jnp.where(cmask, s, NEG)
        e = jnp.exp(s - jnp.max(s, axis=0, keepdims=True))
        p = (e / jnp.sum(e, axis=0, keepdims=True)) * any_c
        pb = p.astype(BF16)
        o_cmp.append(_dot(vct_ref[...], pb))
        imp_t = imp_t + _dot(ovt_ref[...], pb)

    jrow = lax.broadcasted_iota(jnp.int32, (LANES, tq), 0)
    cur = (t0 + lax.broadcasted_iota(jnp.int32, (LANES, tq), 1)) // SLC_LEN
    forced = ((jrow == 0) | (jrow == cur) | (jrow == cur - 1)).astype(F32)
    imp_t = jnp.where(jrow <= cur, imp_t + FORCE_BONUS * forced, -jnp.inf)
    n_sel = svt_ref.shape[1] // SLC_LEN
    rank = jnp.zeros((LANES, tq), jnp.int32)
    for jp in range(n_sel):
        other = imp_t[jp:jp + 1, :]
        ahead = (other > imp_t) | ((other == imp_t) & (jp < jrow))
        rank = rank + ahead.astype(jnp.int32)
    selt_ref[...] = (rank < SLC_TOPK).astype(F32)

    KB = ATTN_KB
    t_pos = t0 + lax.broadcasted_iota(jnp.int32, (KB, tq), 1)
    s_in = lax.broadcasted_iota(jnp.int32, (KB, tq), 0)
    per_kb = KB // SLC_LEN
    m_ref[...] = jnp.full(m_ref.shape, -jnp.inf, F32)
    l_ref[...] = jnp.zeros(l_ref.shape, F32)
    acc_ref[...] = jnp.zeros(acc_ref.shape, F32)

    def sel_body(kb, carry):
        start = pl.multiple_of(kb * KB, KB)
        dist_i = t_pos - (start + s_in)
        dist = dist_i.astype(F32)
        chosen = jnp.concatenate(
            [jnp.broadcast_to(selt_ref[pl.ds(kb * per_kb + c, 1), :], (SLC_LEN, tq)) for c in range(per_kb)], axis=0)
        mask = (dist_i >= 0) & (chosen > 0.5)
        st = _dot_nt(nk_ref[pl.ds(start, KB), 0:LANES], q4)
        vt = svt_ref[:, pl.ds(start, KB)]
        for r in range(R):
            s = st[:, r * tq:(r + 1) * tq] - slope_ref[r:r + 1, 0:1] * dist
            s = jnp.where(mask, s, NEG)
            m_old = m_ref[r:r + 1, :]
            m_new = jnp.maximum(m_old, jnp.max(s, axis=0, keepdims=True))
            alpha = jnp.exp(m_old - m_new)
            p = jnp.exp(s - m_new)
            l_ref[r:r + 1, :] = alpha * l_ref[r:r + 1, :] + jnp.sum(p, axis=0, keepdims=True)
            acc_ref[r] = alpha * acc_ref[r] + _dot(vt, p.astype(BF16))
            m_ref[r:r + 1, :] = m_new
        return carry

    lax.fori_loop(0, t0 // KB + 1, sel_body, 0)

    nwb = (WINDOW + tq) // tq
    w_first = i - (nwb - 1)
    starts = [pl.multiple_of(jnp.maximum(w_first + d, 0) * tq, tq) for d in range(nwb)]
    kw = jnp.concatenate([nk_ref[pl.ds(st_, tq), LANES:2 * LANES] for st_ in starts], axis=0)
    vtw = jnp.concatenate([wvt_ref[:, pl.ds(st_, tq)] for st_ in starts], axis=1)
    wl = nwb * tq
    key_pos = w_first * tq + lax.broadcasted_iota(jnp.int32, (wl, tq), 0)
    wdist_i = (t0 + lax.broadcasted_iota(jnp.int32, (wl, tq), 1)) - key_pos
    wmask = (wdist_i >= 0) & (wdist_i < WINDOW) & (key_pos >= 0)
    wdist = wdist_i.astype(F32)
    stw = _dot_nt(kw, q4)
    o_win = []
    for r in range(R):
        s = jnp.where(wmask, stw[:, r * tq:(r + 1) * tq] - slope_ref[r:r + 1, 0:1] * wdist, NEG)
        e = jnp.exp(s - jnp.max(s, axis=0, keepdims=True))
        p = e / jnp.sum(e, axis=0, keepdims=True)
        o_win.append(_dot(vtw, p.astype(BF16)))

    gates = _sigmoid(gate_ref[...])
    row = lax.broadcasted_iota(jnp.int32, (LANES, tq), 0)
    mine = (row // HEAD_DIM) == g
    for r in range(R):
        ot = (gates[3 * r:3 * r + 1, :] * o_cmp[r] + gates[3 * r + 1:3 * r + 2, :] * (acc_ref[r] / l_ref[r:r + 1, :])
              + gates[3 * r + 2:3 * r + 3, :] * o_win[r])
        ot = jnp.where(mine, ot, 0.0)
        for blk in range(tq // LANES):
            o_ref[blk * LANES:(blk + 1) * LANES, r * LANES:(r + 1) * LANES] = (
                ot[:, blk * LANES:(blk + 1) * LANES].T.astype(o_ref.dtype))


def _alibi_slopes(n):
    return np.asarray(2.0 ** (-8.0 * np.arange(1, n + 1) / n), np.float32)


def nsa_attention(nq, kc, vct, nk, svt, wvt, gates_t, tq=128):
    b, s, _ = nk.shape
    n_cmp = s // CMP_STRIDE
    n_sel = s // SLC_LEN
    slopes = np.broadcast_to(_alibi_slopes(N_NSA_HEADS).reshape(N_NSA_KV, NSA_GROUP, 1), (N_NSA_KV, NSA_GROUP, LANES))
    slopes = jnp.asarray(np.concatenate([slopes, np.zeros((N_NSA_KV, SUBLANES - NSA_GROUP, LANES), np.float32)], axis=1))
    cstart = CMP_STRIDE * np.arange(n_cmp)
    sstart = SLC_LEN * np.arange(LANES)
    ov = ((cstart[None, :] < sstart[:, None] + SLC_LEN) & (cstart[None, :] + CMP_LEN > sstart[:, None])
          & (np.arange(LANES)[:, None] < n_sel) & (cstart[None, :] + CMP_LEN <= s))
    ovt = jnp.asarray(ov, BF16)
    return pl.pallas_call(
        functools.partial(_nsa_attn_kernel, tq=tq),
        out_shape=jax.ShapeDtypeStruct((b, s, N_NSA_HEADS * LANES), BF16),
        grid=(b, N_NSA_KV, s // tq),
        in_specs=[pl.BlockSpec((None, tq, NSA_GROUP * LANES), lambda bb, g, i: (bb, i, g)),
                  pl.BlockSpec((None, n_cmp, LANES), lambda bb, g, i: (bb, 0, 0)),
                  pl.BlockSpec((None, LANES, n_cmp), lambda bb, g, i: (bb, 0, 0)),
                  pl.BlockSpec((None, s, 2 * LANES), lambda bb, g, i: (bb, 0, 0)),
                  pl.BlockSpec((LANES, s), lambda bb, g, i: (0, bb)),
                  pl.BlockSpec((LANES, s), lambda bb, g, i: (0, bb)),
                  pl.BlockSpec((None, None, 2 * SUBLANES, tq), lambda bb, g, i: (bb, g, 0, i)),
                  pl.BlockSpec((None, SUBLANES, LANES), lambda bb, g, i: (g, 0, 0)),
                  pl.BlockSpec((LANES, n_cmp), lambda bb, g, i: (0, 0))],
        out_specs=pl.BlockSpec((None, tq, NSA_GROUP * LANES), lambda bb, g, i: (bb, i, g)),
        scratch_shapes=[pltpu.VMEM((NSA_GROUP * tq, LANES), BF16),
                        pltpu.VMEM((NSA_GROUP, tq), F32), pltpu.VMEM((NSA_GROUP, tq), F32),
                        pltpu.VMEM((NSA_GROUP, LANES, tq), F32), pltpu.VMEM((LANES, tq), F32)],
        compiler_params=_cparams("parallel", "parallel", "arbitrary"),
        name="nsa_attention",
    )(nq, kc, vct, nk, svt, wvt, gates_t, slopes, ovt)


def _peer_cand_layout():
    K = PEER_TOPK
    groups = [((0, 1), (0, K))]
    groups += [((j1, j1 + 1), (0, SUBLANES)) for j1 in range(1, SUBLANES)]
    groups += [((SUBLANES, K), (0, 1))]
    flat = []
    for (a0, a1), (b0, b1) in groups:
        flat += [j1 * K + j2 for j1 in range(a0, a1) for j2 in range(b0, b1)]
    return groups, np.asarray(flat, np.int32)


def _extract_top(vals, tags, n, write):
    big = jnp.int32(2 ** 30)
    for it in range(n):
        m = jnp.max(vals, axis=0, keepdims=True)
        tag = jnp.min(jnp.where(vals == m, tags, big), axis=0, keepdims=True)
        write(it, m, tag)
        vals = jnp.where(tags == tag, -jnp.inf, vals)


def _peer_topk_kernel(q_ref, keys_ref, flat_ref, a_ref, b_ref, g_ref,
                      s1_ref, s2_ref, i1_ref, i2_ref, bs_ref, bid_ref, at_ref, bt_ref, gt_ref, *, tm, groups):
    K = PEER_TOPK
    kidx = lax.broadcasted_iota(jnp.int32, (PEER_NKEYS, tm), 0)
    flat = flat_ref[...]
    for h in range(PEER_HEADS):
        for c, (s_ref, i_ref) in enumerate(((s1_ref, i1_ref), (s2_ref, i2_ref))):
            blk = (2 * h + c) * LANES
            scores = _dot_nt(keys_ref[h, c], q_ref[:, blk:blk + LANES])

            def write(it, m, tag, s_ref=s_ref, i_ref=i_ref):
                s_ref[it:it + 1, :] = m
                i_ref[it:it + 1, :] = tag

            _extract_top(scores, kidx, K, write)
        s1, s2, i1, i2 = s1_ref[...], s2_ref[...], i1_ref[...], i2_ref[...]
        cand, ids = [], []
        for (a0, a1), (b0, b1) in groups:
            cand.append(s1[a0:a1] + s2[b0:b1])
            ids.append(i1[a0:a1] * PEER_NKEYS + i2[b0:b1])
        cand = jnp.concatenate(cand, axis=0)
        ids = jnp.concatenate(ids, axis=0)

        def write2(it, m, tag):
            bs_ref[it:it + 1, :] = m
            bid_ref[it:it + 1, :] = jnp.max(jnp.where(flat == tag, ids, -1), axis=0, keepdims=True)

        _extract_top(cand, flat, K, write2)
        bs = bs_ref[...]
        e = jnp.exp(bs - jnp.max(bs, axis=0, keepdims=True))
        gt_ref[h * K:(h + 1) * K, :] = e / jnp.sum(e, axis=0, keepdims=True)
        bid = bid_ref[...]
        at_ref[h * K:(h + 1) * K, :] = bid // PEER_NKEYS
        bt_ref[h * K:(h + 1) * K, :] = bid % PEER_NKEYS
    for blk in range(tm // LANES):
        sl = slice(blk * LANES, (blk + 1) * LANES)
        a_ref[sl, :] = at_ref[:, sl].T
        b_ref[sl, :] = bt_ref[:, sl].T
        g_ref[sl, :] = gt_ref[:, sl].T


def peer_topk(q_bf16, sub_keys, tm=256):
    n = q_bf16.shape[0]
    groups, flat = _peer_cand_layout()
    nc = flat.shape[0]
    flat_b = jnp.asarray(np.broadcast_to(flat[:, None], (nc, tm)))
    hk = PEER_HEADS * PEER_TOPK
    row = lambda dt: jax.ShapeDtypeStruct((n, hk), dt)
    return pl.pallas_call(
        functools.partial(_peer_topk_kernel, tm=tm, groups=groups),
        out_shape=[row(jnp.int32), row(jnp.int32), row(F32)],
        grid=(n // tm,),
        in_specs=[pl.BlockSpec((tm, q_bf16.shape[1]), lambda i: (i, 0)),
                  pl.BlockSpec(sub_keys.shape, lambda i: (0, 0, 0, 0)),
                  pl.BlockSpec((nc, tm), lambda i: (0, 0))],
        out_specs=[pl.BlockSpec((tm, hk), lambda i: (i, 0))] * 3,
        scratch_shapes=[pltpu.VMEM((PEER_TOPK, tm), F32), pltpu.VMEM((PEER_TOPK, tm), F32),
                        pltpu.VMEM((PEER_TOPK, tm), jnp.int32), pltpu.VMEM((PEER_TOPK, tm), jnp.int32),
                        pltpu.VMEM((PEER_TOPK, tm), F32), pltpu.VMEM((PEER_TOPK, tm), jnp.int32),
                        pltpu.VMEM((hk, tm), jnp.int32), pltpu.VMEM((hk, tm), jnp.int32),
                        pltpu.VMEM((hk, tm), F32)],
        compiler_params=_cparams("parallel"),
        name="peer_topk",
    )(q_bf16, sub_keys.astype(BF16), flat_b)


_W_ROWS = PEER_NKEYS // 2
_W_PITCH = _W_ROWS + SUBLANES


def _peer_apply_kernel(x_ref, ut_ref, v_ref, a_ref, b_ref, g_ref, r_ref, o_ref, w_ref, *, tm, ec, unroll):
    c = pl.program_id(1)
    slabs = ec // PEER_NKEYS
    steps = _W_ROWS // slabs
    half = c // steps
    cc = c % steps

    @pl.when(c == 0)
    def _():
        o_ref[...] = r_ref[...]

    @pl.when(cc == 0)
    def _():
        sub_a = half * _W_ROWS + lax.broadcasted_iota(jnp.int32, (_W_ROWS, LANES), 0)
        sub_b = lax.broadcasted_iota(jnp.int32, (PEER_NKEYS, LANES), 0)

        def build(tb, carry):
            for tt in range(unroll):
                t = tb * unroll + tt
                a = a_ref[pl.ds(t, 1), :]
                b = b_ref[pl.ds(t, 1), :]
                g = g_ref[pl.ds(t, 1), :]
                pt = jnp.where(sub_a == a, g, 0.0).astype(BF16)
                qt = jnp.where(sub_b == b, 1.0, 0.0).astype(BF16)
                w_ref[pl.ds(pl.multiple_of(t * _W_PITCH, SUBLANES), _W_ROWS), :] = _dot_nt(pt, qt)
            return carry

        lax.fori_loop(0, tm // unroll, build, 0)

    act = _dot(x_ref[...], ut_ref[...])
    w = jnp.concatenate([w_ref[pl.ds(cc * slabs + s, tm, stride=_W_PITCH), :] for s in range(slabs)], axis=1)
    o_ref[...] += _dot((_gelu(act) * w).astype(BF16), v_ref[...])


def peer_apply(xn_bf16, ut_bf16, v_bf16, a_idx, b_idx, gw, resid, tm=512, ec=1024, unroll=16):
    n, d = xn_bf16.shape
    ne = v_bf16.shape[0]
    assert ne == PEER_NKEYS * PEER_NKEYS and _W_ROWS % (ec // PEER_NKEYS) == 0
    tok = lambda w: pl.BlockSpec((tm, w), lambda i, c: (i, 0))
    return pl.pallas_call(
        functools.partial(_peer_apply_kernel, tm=tm, ec=ec, unroll=unroll),
        out_shape=jax.ShapeDtypeStruct((n, d), F32),
        grid=(n // tm, ne // ec),
        in_specs=[tok(d),
                  pl.BlockSpec((d, ec), lambda i, c: (0, c)),
                  pl.BlockSpec((ec, d), lambda i, c: (c, 0)),
                  tok(a_idx.shape[1]), tok(b_idx.shape[1]), tok(gw.shape[1]), tok(d)],
        out_specs=tok(d),
        scratch_shapes=[pltpu.VMEM((tm * _W_PITCH, LANES), F32)],
        compiler_params=pltpu.CompilerParams(dimension_semantics=("parallel", "arbitrary"),
                                             vmem_limit_bytes=56 * 1024 * 1024),
        name="peer_apply",
    )(xn_bf16, ut_bf16, v_bf16, a_idx, b_idx, gw, resid)


def _odd_mixer_kernel(h_ref, g_ref, win_ref, cw_ref, cb_ref, wra_ref, bra_ref, wri_ref, bri_ref, lam_ref, wout_ref,
                      o_ref, xbuf_ref, carry_ref, *, tc):
    j = pl.program_id(1)
    d_rnn = cw_ref.shape[1]
    pad = SUBLANES

    @pl.when(j == 0)
    def _():
        xbuf_ref[0:pad, :] = jnp.zeros((pad, d_rnn), F32)
        carry_ref[...] = jnp.zeros(carry_ref.shape, F32)

    h_in = h_ref[...]
    z = _dot(_rmsnorm_rows(h_in, g_ref[...]).astype(BF16), win_ref[...])
    gate = z[:, :d_rnn]
    xbuf_ref[pad:pad + tc, :] = z[:, d_rnn:]
    xc = cb_ref[...] + cw_ref[0:1, :] * xbuf_ref[pl.ds(pad - 3, tc), :]
    for w in range(1, CONV_W):
        xc = xc + cw_ref[w:w + 1, :] * xbuf_ref[pl.ds(pad - 3 + w, tc), :]
    xbuf_ref[0:pad, :] = xbuf_ref[tc:tc + pad, :]

    xcb = xc.astype(BF16)
    nb = d_rnn // RNN_BLOCK_W
    blk = lambda w_ref: jnp.concatenate(
        [_dot(xcb[:, n * RNN_BLOCK_W:(n + 1) * RNN_BLOCK_W], w_ref[n]) for n in range(nb)], axis=1)
    r = _sigmoid(blk(wra_ref) + bra_ref[...])
    gi = _sigmoid(blk(wri_ref) + bri_ref[...])
    nl = -lam_ref[...]
    softplus = jnp.maximum(nl, 0.0) + jnp.log(1.0 + jnp.exp(-jnp.abs(nl)))
    log_a = (-LRU_C) * r * softplus
    a = jnp.exp(log_a)
    u = jnp.sqrt(1.0 - jnp.exp(2.0 * log_a)) * (gi * xc)

    row = lax.broadcasted_iota(jnp.int32, (tc, d_rnn), 0)
    shift = 1
    while shift < tc:
        valid = row >= shift
        a_prev = jnp.where(valid, pltpu.roll(a, shift, 0), 1.0)
        u_prev = jnp.where(valid, pltpu.roll(u, shift, 0), 0.0)
        u = u + a * u_prev
        a = a * a_prev
        shift *= 2
    hs = u + a * carry_ref[...]
    carry_ref[...] = hs[tc - 1:tc, :]

    y = (_gelu(gate) * hs).astype(BF16)
    o_ref[...] = h_in + _dot(y, wout_ref[...])


def odd_mixer_residual(h, norm_g, w_in, conv_w, conv_b, w_ra, b_ra, w_ri, b_ri, lam, w_out, tc=256):
    b, s, d = h.shape
    d_rnn = conv_w.shape[1]
    full = lambda a: pl.BlockSpec(a.shape, lambda bb, j, nd=a.ndim: (0,) * nd)
    row = lambda v: v.astype(F32).reshape(1, -1)
    args = [norm_g.astype(F32).reshape(1, d), w_in.astype(BF16), conv_w.astype(F32), row(conv_b),
            w_ra.astype(BF16), row(b_ra), w_ri.astype(BF16), row(b_ri), row(lam), w_out.astype(BF16)]
    return pl.pallas_call(
        functools.partial(_odd_mixer_kernel, tc=tc),
        out_shape=jax.ShapeDtypeStruct((b, s, d), F32),
        grid=(b, s // tc),
        in_specs=[pl.BlockSpec((None, tc, d), lambda bb, j: (bb, j, 0))] + [full(a) for a in args],
        out_specs=pl.BlockSpec((None, tc, d), lambda bb, j: (bb, j, 0)),
        scratch_shapes=[pltpu.VMEM((tc + 2 * SUBLANES, d_rnn), F32), pltpu.VMEM((1, d_rnn), F32)],
        compiler_params=_cparams("parallel", "arbitrary"),
        name="odd_mixer",
    )(h, *args)


def _rmsnorm_kernel(x_ref, g_ref, o_ref):
    o_ref[...] = _rmsnorm_rows(x_ref[...], g_ref[...])


def rmsnorm_rows(x2d, g, tm=512):
    n, d = x2d.shape
    return pl.pallas_call(
        _rmsnorm_kernel,
        out_shape=jax.ShapeDtypeStruct((n, d), F32),
        grid=(n // tm,),
        in_specs=[pl.BlockSpec((tm, d), lambda i: (i, 0)), pl.BlockSpec((1, d), lambda i: (0, 0))],
        out_specs=pl.BlockSpec((tm, d), lambda i: (i, 0)),
        compiler_params=_cparams("parallel"),
        name="final_rmsnorm",
    )(x2d, g.astype(F32).reshape(1, d))


def even_mixer_residual(h, norm_g, w_in, b_f, cmp_pos, w_cmp_k, w_cmp_v, w_out):
    b, s, d = h.shape
    h2d = h.reshape(b * s, d)
    w_ext, wt_ext = _even_w_ext(w_in)
    outs = norm_proj(h2d, norm_g.astype(F32), w_ext, _EVEN_SEGS, _EVEN_DTYPES, wt_ext, _EVEN_TSEGS, name="even_in_proj")
    fq, fk, nq, ncmp, nk, small = [o.reshape(b, s, -1) for o in outs[:6]]
    fvt, svt, wvt = outs[6:]
    c = fox_gate_cumsum(small[..., :N_FOX_HEADS].transpose(0, 2, 1), b_f)
    o_fox = fox_attention(fq, fk, fvt, c)
    kc, vct = nsa_compress(ncmp, cmp_pos, w_cmp_k, w_cmp_v)
    gates_t = small[..., N_FOX_HEADS:N_FOX_HEADS + NSA_GATES].reshape(b, s, N_NSA_KV, 3 * NSA_GROUP).transpose(0, 2, 3, 1)
    gates_t = jnp.pad(gates_t, ((0, 0), (0, 0), (0, 2 * SUBLANES - 3 * NSA_GROUP), (0, 0)))
    o_nsa = nsa_attention(nq, kc, vct, nk, svt, wvt, gates_t)
    nf = N_FOX_HEADS * HEAD_DIM
    w_n = w_out[nf:].reshape(N_NSA_HEADS, HEAD_DIM, d)
    half_n = (np.arange(N_NSA_HEADS) // NSA_GROUP)[:, None, None, None] == np.arange(2)[None, :, None, None]
    w_n = jnp.where(half_n, w_n[:, None], 0.0).reshape(N_NSA_HEADS * LANES, d)
    pairs = [(o_fox.reshape(b * s, -1), w_out[:nf].astype(BF16)), (o_nsa.reshape(b * s, -1), w_n.astype(BF16))]
    return proj_residual(pairs, h2d, name="even_out_proj").reshape(b, s, d)


def peer_ffn_residual(h2d, norm_g, w_q, sub_keys, u_tab, v_tab):
    d = h2d.shape[1]
    nq = w_q.shape[1]
    q, xn = norm_proj(h2d, norm_g, w_q.astype(BF16), ((0, nq, 1.0), (None, d, 1.0)), (BF16, BF16), name="peer_query")
    a_idx, b_idx, gw = peer_topk(q, sub_keys)
    return peer_apply(xn, u_tab.astype(BF16).T, v_tab.astype(BF16), a_idx, b_idx, gw, h2d)


def kernel(x, even_norm_g, even_w_in, even_b_f, even_cmp_pos, even_w_cmp_k, even_w_cmp_v, even_w_out, odd_norm_g, odd_w_in, odd_conv_w, odd_conv_b, odd_w_ra, odd_b_ra, odd_w_ri, odd_b_ri, odd_lam, odd_w_out, ffn_norm_g, peer_w_q, peer_sub_keys, peer_u, peer_v, final_g):
    b, s, d = x.shape
    depth = ffn_norm_g.shape[0]
    h = x
    for layer in range(depth):
        j = layer // 2
        if layer % 2 == 0:
            h = even_mixer_residual(h, even_norm_g[j], even_w_in[j], even_b_f[j], even_cmp_pos[j],
                                    even_w_cmp_k[j], even_w_cmp_v[j], even_w_out[j])
        else:
            h = odd_mixer_residual(h, odd_norm_g[j], odd_w_in[j], odd_conv_w[j], odd_conv_b[j], odd_w_ra[j],
                                   odd_b_ra[j], odd_w_ri[j], odd_b_ri[j], odd_lam[j], odd_w_out[j])
        h = peer_ffn_residual(h.reshape(b * s, d), ffn_norm_g[layer].astype(F32), peer_w_q[layer],
                              peer_sub_keys[layer], peer_u[layer], peer_v[layer]).reshape(b, s, d)
    return rmsnorm_rows(h.reshape(b * s, d), final_g).reshape(b, s, d)
```

```python
import functools
import math

import jax
import jax.numpy as jnp
import numpy as np
from jax import lax
from jax.experimental import pallas as pl
from jax.experimental.pallas import tpu as pltpu

HEAD_DIM = 64
N_FOX_HEADS = 8
N_NSA_HEADS = 8
N_NSA_KV = 2
NSA_GROUP = N_NSA_HEADS // N_NSA_KV
CMP_LEN = 32
CMP_STRIDE = 16
SLC_LEN = 64
SLC_TOPK = 16
WINDOW = 512
FORCE_BONUS = 1.0e4
NEG = -1.0e30
D_RNN = 1280
RNN_BLOCKS = 10
RNN_BLOCK_W = D_RNN // RNN_BLOCKS
CONV_W = 4
LRU_C = 8.0
PEER_HEADS = 8
PEER_NKEYS = 128
PEER_HALF = 128
PEER_TOPK = 16
RMS_EPS = 1e-6

LANES = 128
SUBLANES = 8
VMEM_LIMIT = 48 * 1024 * 1024

F32 = jnp.float32
BF16 = jnp.bfloat16


def _cparams(*sem):
    return pltpu.CompilerParams(dimension_semantics=sem, vmem_limit_bytes=VMEM_LIMIT)


def _dot(a, b):
    return lax.dot_general(a, b, (((1,), (0,)), ((), ())), preferred_element_type=F32)


def _dot_nt(a, b):
    return lax.dot_general(a, b, (((1,), (1,)), ((), ())), preferred_element_type=F32)


def _rmsnorm_rows(x, g):
    return x * lax.rsqrt(jnp.mean(x * x, axis=-1, keepdims=True) + RMS_EPS) * g


def _gelu(x):
    return 0.5 * x * (1.0 + lax.erf(x * (1.0 / math.sqrt(2.0))))


def _sigmoid(x):
    return 1.0 / (1.0 + jnp.exp(-x))


def _norm_proj_kernel(x_ref, g_ref, w_ref, wt_ref, *out_refs, segs, tsegs):
    xn = _rmsnorm_rows(x_ref[...], g_ref[...]).astype(BF16)
    for o_ref, (start, width, scale) in zip(out_refs, segs):
        if start is None:
            o_ref[...] = xn
            continue
        y = _dot(xn, w_ref[:, start:start + width])
        if scale != 1.0:
            y = y * scale
        o_ref[...] = y.astype(o_ref.dtype)
    for o_ref, (start, width) in zip(out_refs[len(segs):], tsegs):
        o_ref[...] = _dot_nt(wt_ref[start:start + width, :], xn).astype(o_ref.dtype)


def norm_proj(x2d, g, w_bf16, segs, dtypes, wt_bf16=None, tsegs=(), tm=256, name="norm_proj"):
    n, d = x2d.shape
    if wt_bf16 is None:
        wt_bf16 = jnp.zeros((SUBLANES, d), BF16)
    outs = [jax.ShapeDtypeStruct((n, wd), dt) for (_, wd, _), dt in zip(segs, dtypes)]
    outs += [jax.ShapeDtypeStruct((wd, n), BF16) for (_, wd) in tsegs]
    return pl.pallas_call(
        functools.partial(_norm_proj_kernel, segs=tuple(segs), tsegs=tuple(tsegs)),
        out_shape=outs,
        grid=(n // tm,),
        in_specs=[pl.BlockSpec((tm, d), lambda i: (i, 0)),
                  pl.BlockSpec((1, d), lambda i: (0, 0)),
                  pl.BlockSpec(w_bf16.shape, lambda i: (0, 0)),
                  pl.BlockSpec(wt_bf16.shape, lambda i: (0, 0))],
        out_specs=([pl.BlockSpec((tm, wd), lambda i: (i, 0)) for (_, wd, _) in segs]
                   + [pl.BlockSpec((wd, tm), lambda i: (0, i)) for (_, wd) in tsegs]),
        compiler_params=_cparams("parallel"),
        name=name,
    )(x2d, g.reshape(1, d), w_bf16, wt_bf16)


def _proj_residual_kernel(*refs):
    r_ref, o_ref = refs[-2], refs[-1]
    acc = r_ref[...]
    for a_ref, w_ref in zip(refs[0:-2:2], refs[1:-2:2]):
        acc = acc + _dot(a_ref[...], w_ref[...])
    o_ref[...] = acc


def proj_residual(pairs, resid, tm=512, name="proj_residual"):
    n, d = resid.shape
    args, specs = [], []
    for a, w in pairs:
        args += [a, w]
        specs += [pl.BlockSpec((tm, a.shape[1]), lambda i: (i, 0)), pl.BlockSpec(w.shape, lambda i: (0, 0))]
    return pl.pallas_call(
        _proj_residual_kernel,
        out_shape=jax.ShapeDtypeStruct((n, d), F32),
        grid=(n // tm,),
        in_specs=specs + [pl.BlockSpec((tm, d), lambda i: (i, 0))],
        out_specs=pl.BlockSpec((tm, d), lambda i: (i, 0)),
        compiler_params=_cparams("parallel"),
        name=name,
    )(*args, resid)


def _fox_gate_kernel(f_ref, b_ref, c_ref):
    z = f_ref[...] + b_ref[...]
    x = jnp.minimum(z, 0.0) - jnp.log(1.0 + jnp.exp(-jnp.abs(z)))
    s = x.shape[-1]
    lane = lax.broadcasted_iota(jnp.int32, x.shape, 1)
    shift = 1
    while shift < s:
        x = x + jnp.where(lane >= shift, pltpu.roll(x, shift, 1), 0.0)
        shift *= 2
    c_ref[...] = x


def fox_gate_cumsum(f_t, b_f):
    b, h, s = f_t.shape
    return pl.pallas_call(
        _fox_gate_kernel,
        out_shape=jax.ShapeDtypeStruct((b, h, s), F32),
        grid=(b,),
        in_specs=[pl.BlockSpec((None, h, s), lambda i: (i, 0, 0)),
                  pl.BlockSpec((h, 1), lambda i: (0, 0))],
        out_specs=pl.BlockSpec((None, h, s), lambda i: (i, 0, 0)),
        compiler_params=_cparams("parallel"),
        name="fox_gate_cumsum",
    )(f_t, b_f.reshape(h, 1).astype(F32))


ATTN_KB = 512


def _fox_attn_kernel(q_ref, k_ref, vt_ref, ct_ref, cs_ref, o_ref, qs_ref, m_ref, l_ref, acc_ref, *, tq):
    KB = ATTN_KB
    i = pl.program_id(2)
    t0 = i * tq
    for hh in range(2):
        qs_ref[hh * tq:(hh + 1) * tq, :] = q_ref[:, hh * LANES:(hh + 1) * LANES]
    qs = qs_ref[...]
    m_ref[...] = jnp.full(m_ref.shape, -jnp.inf, F32)
    l_ref[...] = jnp.zeros(l_ref.shape, F32)
    acc_ref[...] = jnp.zeros(acc_ref.shape, F32)
    t_pos = t0 + lax.broadcasted_iota(jnp.int32, (KB, tq), 1)
    s_in = lax.broadcasted_iota(jnp.int32, (KB, tq), 0)

    def step(kb, masked):
        start = pl.multiple_of(kb * KB, KB)
        st = _dot_nt(k_ref[pl.ds(start, KB), :], qs)
        vt = vt_ref[:, pl.ds(start, KB)]
        cs = cs_ref[pl.ds(start, KB), :]
        for hh in range(2):
            s = st[:, hh * tq:(hh + 1) * tq] + (ct_ref[hh:hh + 1, :] - cs[:, hh:hh + 1])
            if masked:
                s = jnp.where(start + s_in <= t_pos, s, NEG)
            m_old = m_ref[hh:hh + 1, :]
            m_new = jnp.maximum(m_old, jnp.max(s, axis=0, keepdims=True))
            alpha = jnp.exp(m_old - m_new)
            p = jnp.exp(s - m_new)
            l_ref[hh:hh + 1, :] = alpha * l_ref[hh:hh + 1, :] + jnp.sum(p, axis=0, keepdims=True)
            acc_ref[hh] = alpha * acc_ref[hh] + _dot(vt, p.astype(BF16))
            m_ref[hh:hh + 1, :] = m_new

    n_full = t0 // KB

    def full_body(kb, carry):
        step(kb, False)
        return carry

    lax.fori_loop(0, n_full, full_body, 0)
    step(n_full, True)

    row = lax.broadcasted_iota(jnp.int32, (LANES, tq), 0)
    o0 = acc_ref[0] / l_ref[0:1, :]
    o1 = acc_ref[1] / l_ref[1:2, :]
    ot = jnp.where(row < HEAD_DIM, o0, o1)
    for blk in range(tq // LANES):
        o_ref[blk * LANES:(blk + 1) * LANES, :] = ot[:, blk * LANES:(blk + 1) * LANES].T.astype(o_ref.dtype)


def fox_attention(fq, fk, fvt, c, tq=256):
    b, s, _ = fk.shape
    hp = N_FOX_HEADS // 2
    ct = c.reshape(b, hp, 2, s)
    cs = ct.transpose(0, 1, 3, 2)
    return pl.pallas_call(
        functools.partial(_fox_attn_kernel, tq=tq),
        out_shape=jax.ShapeDtypeStruct((b, s, N_FOX_HEADS * HEAD_DIM), BF16),
        grid=(b, hp, s // tq),
        in_specs=[pl.BlockSpec((None, tq, 2 * LANES), lambda bb, p, i: (bb, i, p)),
                  pl.BlockSpec((None, s, LANES), lambda bb, p, i: (bb, 0, p)),
                  pl.BlockSpec((LANES, s), lambda bb, p, i: (p, bb)),
                  pl.BlockSpec((None, None, 2, tq), lambda bb, p, i: (bb, p, 0, i)),
                  pl.BlockSpec((None, None, s, 2), lambda bb, p, i: (bb, p, 0, 0))],
        out_specs=pl.BlockSpec((None, tq, LANES), lambda bb, p, i: (bb, i, p)),
        scratch_shapes=[pltpu.VMEM((2 * tq, LANES), BF16), pltpu.VMEM((2, tq), F32), pltpu.VMEM((2, tq), F32),
                        pltpu.VMEM((2, LANES, tq), F32)],
        compiler_params=_cparams("parallel", "parallel", "arbitrary"),
        name="fox_attention",
    )(fq, fk, fvt, ct, cs)


FOX_QKV = 3 * N_FOX_HEADS * HEAD_DIM
NSA_Q = N_NSA_HEADS * HEAD_DIM
NSA_KV = 6 * N_NSA_KV * HEAD_DIM
NSA_GATES = 3 * N_NSA_HEADS
_QK_SCALE = HEAD_DIM ** -0.5

_EVEN_SEGS = ((0, 1024, _QK_SCALE), (1024, 512, 1.0), (1536, 1024, _QK_SCALE),
              (2560, 256, 1.0), (2816, 256, 1.0), (3072, 128, 1.0))
_EVEN_DTYPES = (BF16, BF16, BF16, F32, BF16, F32)
_EVEN_TSEGS = ((0, 512), (512, 128), (640, 128))


def _even_w_ext(w_in):
    d = w_in.shape[0]
    o = 0
    w_fq = w_in[:, o:o + 512].reshape(d, N_FOX_HEADS, HEAD_DIM); o += 512
    w_fk = w_in[:, o:o + 512]; o += 512
    w_fv = w_in[:, o:o + 512]; o += 512
    w_ff = w_in[:, o:o + N_FOX_HEADS]; o += N_FOX_HEADS
    w_nq = w_in[:, o:o + NSA_Q].reshape(d, N_NSA_HEADS, HEAD_DIM); o += NSA_Q
    w_nkv = w_in[:, o:o + NSA_KV]; o += NSA_KV
    w_ng = w_in[:, o:o + NSA_GATES]
    half_f = (np.arange(N_FOX_HEADS) % 2)[None, :, None, None] == np.arange(2)[None, None, :, None]
    fq = jnp.where(half_f, w_fq[:, :, None, :], 0.0).reshape(d, N_FOX_HEADS * LANES)
    half_n = (np.arange(N_NSA_HEADS) // NSA_GROUP)[None, :, None, None] == np.arange(2)[None, None, :, None]
    nq = jnp.where(half_n, w_nq[:, :, None, :], 0.0).reshape(d, N_NSA_HEADS * LANES)
    small = jnp.concatenate([w_ff, w_ng, jnp.zeros((d, LANES - N_FOX_HEADS - NSA_GATES), w_in.dtype)], axis=1)
    w = jnp.concatenate([fq, w_fk, nq, w_nkv[:, 0:256], w_nkv[:, 256:384], w_nkv[:, 512:640], small], axis=1)
    wt = jnp.concatenate([w_fv, w_nkv[:, 384:512], w_nkv[:, 640:768]], axis=1).T
    return w.astype(BF16), wt.astype(BF16)


def _nsa_compress_kernel(x_ref, pos_ref, w_ref, kc_ref, vct_ref):
    n = x_ref.shape[2]
    for kind in range(2):
        acc = jnp.zeros((n, LANES), F32)
        for g in range(N_NSA_KV):
            x = x_ref[kind, g]
            top = _dot((x + pos_ref[0:1, :]).astype(BF16), w_ref[kind, g, 0])
            bot = _dot((x + pos_ref[1:2, :]).astype(BF16), w_ref[kind, g, 1])
            acc = acc + top + pltpu.roll(bot, n - 1, 0)
        if kind == 0:
            kc_ref[...] = acc.astype(kc_ref.dtype)
        else:
            for blk in range(n // LANES):
                sl = slice(blk * LANES, (blk + 1) * LANES)
                vct_ref[:, sl] = acc[sl, :].T.astype(vct_ref.dtype)


def nsa_compress(ncmp, cmp_pos, w_cmp_k, w_cmp_v):
    b, s, _ = ncmp.shape
    n = s // CMP_STRIDE
    half = CMP_STRIDE * HEAD_DIM
    x = ncmp.reshape(b, n, CMP_STRIDE, 2, N_NSA_KV, HEAD_DIM).transpose(0, 3, 4, 1, 2, 5).reshape(b, 2, N_NSA_KV, n, half)
    pos = cmp_pos.astype(F32).reshape(2, half)
    w = jnp.stack([w_cmp_k, w_cmp_v]).astype(F32).reshape(2, 2, half, HEAD_DIM)
    place = (np.arange(N_NSA_KV)[:, None, None] == np.arange(2)[None, :, None])
    w_pad = jnp.where(place[None, :, None, None], w[:, None, :, :, None, :], 0.0)
    w_pad = w_pad.reshape(2, N_NSA_KV, 2, half, LANES).astype(BF16)
    return pl.pallas_call(
        _nsa_compress_kernel,
        out_shape=[jax.ShapeDtypeStruct((b, n, LANES), BF16), jax.ShapeDtypeStruct((b, LANES, n), BF16)],
        grid=(b,),
        in_specs=[pl.BlockSpec((None, 2, N_NSA_KV, n, half), lambda i: (i, 0, 0, 0, 0)),
                  pl.BlockSpec((2, half), lambda i: (0, 0)),
                  pl.BlockSpec((2, N_NSA_KV, 2, half, LANES), lambda i: (0, 0, 0, 0, 0))],
        out_specs=[pl.BlockSpec((None, n, LANES), lambda i: (i, 0, 0)),
                   pl.BlockSpec((None, LANES, n), lambda i: (i, 0, 0))],
        compiler_params=_cparams("parallel"),
        name="nsa_compress",
    )(x, pos, w_pad)


def _nsa_attn_kernel(q_ref, kc_ref, vct_ref, nk_ref, svt_ref, wvt_ref, gate_ref, slope_ref, ovt_ref, o_ref,
                     q4_ref, m_ref, l_ref, acc_ref, selt_ref, *, tq):
    g = pl.program_id(1)
    i = pl.program_id(2)
    t0 = i * tq
    n_cmp = kc_ref.shape[0]
    R = NSA_GROUP

    for r in range(R):
        q4_ref[r * tq:(r + 1) * tq, :] = q_ref[:, r * LANES:(r + 1) * LANES]
    q4 = q4_ref[...]

    t_c = t0 + lax.broadcasted_iota(jnp.int32, (n_cmp, tq), 1)
    cend = CMP_STRIDE * lax.broadcasted_iota(jnp.int32, (n_cmp, tq), 0) + (CMP_LEN - 1)
    cmask = cend <= t_c
    cdist = (t_c - cend).astype(F32)
    any_c = (t_c[0:1, :] >= CMP_LEN - 1).astype(F32)
    s_c = _dot_nt(kc_ref[...], q4)
    imp_t = jnp.zeros((ovt_ref.shape[0], tq), F32)
    o_cmp = []
    for r in range(R):
        s = s_c[:, r * tq:(r + 1) * tq] - slope_ref[r:r + 1, 0:1] * cdist
        s = jnp.where(cmask, s, NEG)
        e = jnp.exp(s - jnp.max(s, axis=0, keepdims=True))
        p = (e / jnp.sum(e, axis=0, keepdims=True)) * any_c
        pb = p.astype(BF16)
        o_cmp.append(_dot(vct_ref[...], pb))
        imp_t = imp_t + _dot(ovt_ref[...], pb)

    n_sel = ovt_ref.shape[0]
    jrow = lax.broadcasted_iota(jnp.int32, (n_sel, tq), 0)
    cur = (t0 + lax.broadcasted_iota(jnp.int32, (n_sel, tq), 1)) // SLC_LEN
    forced = ((jrow == 0) | (jrow == cur) | (jrow == cur - 1)).astype(F32)
    imp_t = jnp.where(jrow <= cur, imp_t + FORCE_BONUS * forced, -jnp.inf)
    ngrp = n_sel // SUBLANES
    grp = [imp_t[gi * SUBLANES:(gi + 1) * SUBLANES, :] for gi in range(ngrp)]
    sub8 = lax.broadcasted_iota(jnp.int32, (SUBLANES, tq), 0)
    rank = [jnp.zeros((SUBLANES, tq), jnp.int32) for _ in range(ngrp)]
    for jp in range(n_sel):
        other = imp_t[jp:jp + 1, :]
        for gi in range(ngrp):
            lo = gi * SUBLANES
            if lo + SUBLANES - 1 < jp:
                ahead = other > grp[gi]
            elif lo > jp:
                ahead = other >= grp[gi]
            else:
                ahead = (other > grp[gi]) | ((other == grp[gi]) & (sub8 > jp - lo))
            rank[gi] = rank[gi] + ahead.astype(jnp.int32)
    for gi in range(ngrp):
        selt_ref[gi * SUBLANES:(gi + 1) * SUBLANES, :] = (rank[gi] < SLC_TOPK).astype(F32)

    KB = ATTN_KB
    t_pos = t0 + lax.broadcasted_iota(jnp.int32, (KB, tq), 1)
    s_in = lax.broadcasted_iota(jnp.int32, (KB, tq), 0)
    per_kb = KB // SLC_LEN
    m_ref[...] = jnp.full(m_ref.shape, -jnp.inf, F32)
    l_ref[...] = jnp.zeros(l_ref.shape, F32)
    acc_ref[...] = jnp.zeros(acc_ref.shape, F32)

    def sel_body(kb, carry):
        start = pl.multiple_of(kb * KB, KB)
        dist_i = t_pos - (start + s_in)
        dist = dist_i.astype(F32)
        chosen = jnp.concatenate(
            [jnp.broadcast_to(selt_ref[pl.ds(kb * per_kb + c, 1), :], (SLC_LEN, tq)) for c in range(per_kb)], axis=0)
        mask = (dist_i >= 0) & (chosen > 0.5)
        st = _dot_nt(nk_ref[pl.ds(start, KB), 0:LANES], q4)
        vt = svt_ref[:, pl.ds(start, KB)]
        for r in range(R):
            s = st[:, r * tq:(r + 1) * tq] - slope_ref[r:r + 1, 0:1] * dist
            s = jnp.where(mask, s, NEG)
            m_old = m_ref[r:r + 1, :]
            m_new = jnp.maximum(m_old, jnp.max(s, axis=0, keepdims=True))
            alpha = jnp.exp(m_old - m_new)
            p = jnp.exp(s - m_new)
            l_ref[r:r + 1, :] = alpha * l_ref[r:r + 1, :] + jnp.sum(p, axis=0, keepdims=True)
            acc_ref[r] = alpha * acc_ref[r] + _dot(vt, p.astype(BF16))
            m_ref[r:r + 1, :] = m_new
        return carry

    lax.fori_loop(0, t0 // KB + 1, sel_body, 0)

    nwb = (WINDOW + tq) // tq
    w_first = i - (nwb - 1)
    starts = [pl.multiple_of(jnp.maximum(w_first + d, 0) * tq, tq) for d in range(nwb)]
    kw = jnp.concatenate([nk_ref[pl.ds(st_, tq), LANES:2 * LANES] for st_ in starts], axis=0)
    vtw = jnp.concatenate([wvt_ref[:, pl.ds(st_, tq)] for st_ in starts], axis=1)
    wl = nwb * tq
    key_pos = w_first * tq + lax.broadcasted_iota(jnp.int32, (wl, tq), 0)
    wdist_i = (t0 + lax.broadcasted_iota(jnp.int32, (wl, tq), 1)) - key_pos
    wmask = (wdist_i >= 0) & (wdist_i < WINDOW) & (key_pos >= 0)
    wdist = wdist_i.astype(F32)
    stw = _dot_nt(kw, q4)
    o_win = []
    for r in range(R):
        s = jnp.where(wmask, stw[:, r * tq:(r + 1) * tq] - slope_ref[r:r + 1, 0:1] * wdist, NEG)
        e = jnp.exp(s - jnp.max(s, axis=0, keepdims=True))
        p = e / jnp.sum(e, axis=0, keepdims=True)
        o_win.append(_dot(vtw, p.astype(BF16)))

    gates = _sigmoid(gate_ref[...])
    row = lax.broadcasted_iota(jnp.int32, (LANES, tq), 0)
    mine = (row // HEAD_DIM) == g
    for r in range(R):
        ot = (gates[3 * r:3 * r + 1, :] * o_cmp[r] + gates[3 * r + 1:3 * r + 2, :] * (acc_ref[r] / l_ref[r:r + 1, :])
              + gates[3 * r + 2:3 * r + 3, :] * o_win[r])
        ot = jnp.where(mine, ot, 0.0)
        for blk in range(tq // LANES):
            o_ref[blk * LANES:(blk + 1) * LANES, r * LANES:(r + 1) * LANES] = (
                ot[:, blk * LANES:(blk + 1) * LANES].T.astype(o_ref.dtype))


def _alibi_slopes(n):
    return np.asarray(2.0 ** (-8.0 * np.arange(1, n + 1) / n), np.float32)


def nsa_attention(nq, kc, vct, nk, svt, wvt, gates_t, tq=128):
    b, s, _ = nk.shape
    n_cmp = s // CMP_STRIDE
    n_sel = s // SLC_LEN
    slopes = np.broadcast_to(_alibi_slopes(N_NSA_HEADS).reshape(N_NSA_KV, NSA_GROUP, 1), (N_NSA_KV, NSA_GROUP, LANES))
    slopes = jnp.asarray(np.concatenate([slopes, np.zeros((N_NSA_KV, SUBLANES - NSA_GROUP, LANES), np.float32)], axis=1))
    cstart = CMP_STRIDE * np.arange(n_cmp)
    sstart = SLC_LEN * np.arange(n_sel)
    ov = ((cstart[None, :] < sstart[:, None] + SLC_LEN) & (cstart[None, :] + CMP_LEN > sstart[:, None])
          & (cstart[None, :] + CMP_LEN <= s))
    ovt = jnp.asarray(ov, BF16)
    return pl.pallas_call(
        functools.partial(_nsa_attn_kernel, tq=tq),
        out_shape=jax.ShapeDtypeStruct((b, s, N_NSA_HEADS * LANES), BF16),
        grid=(b, N_NSA_KV, s // tq),
        in_specs=[pl.BlockSpec((None, tq, NSA_GROUP * LANES), lambda bb, g, i: (bb, i, g)),
                  pl.BlockSpec((None, n_cmp, LANES), lambda bb, g, i: (bb, 0, 0)),
                  pl.BlockSpec((None, LANES, n_cmp), lambda bb, g, i: (bb, 0, 0)),
                  pl.BlockSpec((None, s, 2 * LANES), lambda bb, g, i: (bb, 0, 0)),
                  pl.BlockSpec((LANES, s), lambda bb, g, i: (0, bb)),
                  pl.BlockSpec((LANES, s), lambda bb, g, i: (0, bb)),
                  pl.BlockSpec((None, None, 2 * SUBLANES, tq), lambda bb, g, i: (bb, g, 0, i)),
                  pl.BlockSpec((None, SUBLANES, LANES), lambda bb, g, i: (g, 0, 0)),
                  pl.BlockSpec((n_sel, n_cmp), lambda bb, g, i: (0, 0))],
        out_specs=pl.BlockSpec((None, tq, NSA_GROUP * LANES), lambda bb, g, i: (bb, i, g)),
        scratch_shapes=[pltpu.VMEM((NSA_GROUP * tq, LANES), BF16),
                        pltpu.VMEM((NSA_GROUP, tq), F32), pltpu.VMEM((NSA_GROUP, tq), F32),
                        pltpu.VMEM((NSA_GROUP, LANES, tq), F32), pltpu.VMEM((n_sel, tq), F32)],
        compiler_params=_cparams("parallel", "parallel", "arbitrary"),
        name="nsa_attention",
    )(nq, kc, vct, nk, svt, wvt, gates_t, slopes, ovt)


def _peer_cand_layout():
    K = PEER_TOPK
    groups = [((0, 1), (0, K))]
    groups += [((j1, j1 + 1), (0, SUBLANES)) for j1 in range(1, SUBLANES)]
    groups += [((SUBLANES, K), (0, 1))]
    flat = []
    for (a0, a1), (b0, b1) in groups:
        flat += [j1 * K + j2 for j1 in range(a0, a1) for j2 in range(b0, b1)]
    return groups, np.asarray(flat, np.int32)


def _extract_top(vals, tags, n, write):
    big = jnp.int32(2 ** 30)
    for it in range(n):
        m = jnp.max(vals, axis=0, keepdims=True)
        tag = jnp.min(jnp.where(vals == m, tags, big), axis=0, keepdims=True)
        write(it, m, tag)
        vals = jnp.where(tags == tag, -jnp.inf, vals)


def _peer_topk_kernel(q_ref, keys_ref, flat_ref, a_ref, b_ref, g_ref,
                      s1_ref, s2_ref, i1_ref, i2_ref, bs_ref, bid_ref, at_ref, bt_ref, gt_ref, *, tm, groups):
    K = PEER_TOPK
    kidx = lax.broadcasted_iota(jnp.int32, (PEER_NKEYS, tm), 0)
    flat = flat_ref[...]
    for h in range(PEER_HEADS):
        for c, (s_ref, i_ref) in enumerate(((s1_ref, i1_ref), (s2_ref, i2_ref))):
            blk = (2 * h + c) * LANES
            scores = _dot_nt(keys_ref[h, c], q_ref[:, blk:blk + LANES])

            def write(it, m, tag, s_ref=s_ref, i_ref=i_ref):
                s_ref[it:it + 1, :] = m
                i_ref[it:it + 1, :] = tag

            _extract_top(scores, kidx, K, write)
        s1, s2, i1, i2 = s1_ref[...], s2_ref[...], i1_ref[...], i2_ref[...]
        cand, ids = [], []
        for (a0, a1), (b0, b1) in groups:
            cand.append(s1[a0:a1] + s2[b0:b1])
            ids.append(i1[a0:a1] * PEER_NKEYS + i2[b0:b1])
        cand = jnp.concatenate(cand, axis=0)
        tags = flat * (PEER_NKEYS * PEER_NKEYS) + jnp.concatenate(ids, axis=0)

        def write2(it, m, tag):
            bs_ref[it:it + 1, :] = m
            bid_ref[it:it + 1, :] = tag

        _extract_top(cand, tags, K, write2)
        bs = bs_ref[...]
        e = jnp.exp(bs - jnp.max(bs, axis=0, keepdims=True))
        gt_ref[h * K:(h + 1) * K, :] = e / jnp.sum(e, axis=0, keepdims=True)
        bid = bid_ref[...]
        at_ref[h * K:(h + 1) * K, :] = (bid // PEER_NKEYS) % PEER_NKEYS
        bt_ref[h * K:(h + 1) * K, :] = bid % PEER_NKEYS
    for blk in range(tm // LANES):
        sl = slice(blk * LANES, (blk + 1) * LANES)
        a_ref[sl, :] = at_ref[:, sl].T
        b_ref[sl, :] = bt_ref[:, sl].T
        g_ref[sl, :] = gt_ref[:, sl].T


def peer_topk(q_bf16, sub_keys, tm=256):
    n = q_bf16.shape[0]
    groups, flat = _peer_cand_layout()
    nc = flat.shape[0]
    flat_b = jnp.asarray(np.broadcast_to(flat[:, None], (nc, tm)))
    hk = PEER_HEADS * PEER_TOPK
    row = lambda dt: jax.ShapeDtypeStruct((n, hk), dt)
    return pl.pallas_call(
        functools.partial(_peer_topk_kernel, tm=tm, groups=groups),
        out_shape=[row(jnp.int32), row(jnp.int32), row(F32)],
        grid=(n // tm,),
        in_specs=[pl.BlockSpec((tm, q_bf16.shape[1]), lambda i: (i, 0)),
                  pl.BlockSpec(sub_keys.shape, lambda i: (0, 0, 0, 0)),
                  pl.BlockSpec((nc, tm), lambda i: (0, 0))],
        out_specs=[pl.BlockSpec((tm, hk), lambda i: (i, 0))] * 3,
        scratch_shapes=[pltpu.VMEM((PEER_TOPK, tm), F32), pltpu.VMEM((PEER_TOPK, tm), F32),
                        pltpu.VMEM((PEER_TOPK, tm), jnp.int32), pltpu.VMEM((PEER_TOPK, tm), jnp.int32),
                        pltpu.VMEM((PEER_TOPK, tm), F32), pltpu.VMEM((PEER_TOPK, tm), jnp.int32),
                        pltpu.VMEM((hk, tm), jnp.int32), pltpu.VMEM((hk, tm), jnp.int32),
                        pltpu.VMEM((hk, tm), F32)],
        compiler_params=_cparams("parallel"),
        name="peer_topk",
    )(q_bf16, sub_keys.astype(BF16), flat_b)


_W_ROWS = PEER_NKEYS
_W_PITCH = _W_ROWS + SUBLANES


def _peer_apply_kernel(x_ref, ut_ref, v_ref, a_ref, b_ref, g_ref, r_ref, o_ref, w_ref, *, tm, ec, unroll):
    c = pl.program_id(1)
    slabs = ec // PEER_NKEYS
    steps = _W_ROWS // slabs
    half = c // steps
    cc = c % steps

    @pl.when(c == 0)
    def _():
        o_ref[...] = r_ref[...]

    @pl.when(cc == 0)
    def _():
        sub_a = half * _W_ROWS + lax.broadcasted_iota(jnp.int32, (_W_ROWS, LANES), 0)
        sub_b = lax.broadcasted_iota(jnp.int32, (PEER_NKEYS, LANES), 0)

        def build(tb, carry):
            for tt in range(unroll):
                t = tb * unroll + tt
                a = a_ref[pl.ds(t, 1), :]
                b = b_ref[pl.ds(t, 1), :]
                g = g_ref[pl.ds(t, 1), :]
                pt = jnp.where(sub_a == a, g, 0.0).astype(BF16)
                qt = jnp.where(sub_b == b, 1.0, 0.0).astype(BF16)
                w_ref[pl.ds(pl.multiple_of(t * _W_PITCH, SUBLANES), _W_ROWS), :] = _dot_nt(pt, qt)
            return carry

        lax.fori_loop(0, tm // unroll, build, 0)

    act = _dot(x_ref[...], ut_ref[...])
    w = jnp.concatenate([w_ref[pl.ds(cc * slabs + s, tm, stride=_W_PITCH), :] for s in range(slabs)], axis=1)
    o_ref[...] += _dot((_gelu(act) * w).astype(BF16), v_ref[...])


def peer_apply(xn_bf16, ut_bf16, v_bf16, a_idx, b_idx, gw, resid, tm=256, ec=2048, unroll=32):
    n, d = xn_bf16.shape
    ne = v_bf16.shape[0]
    assert ne == PEER_NKEYS * PEER_NKEYS and _W_ROWS % (ec // PEER_NKEYS) == 0
    tok = lambda w: pl.BlockSpec((tm, w), lambda i, c: (i, 0))
    return pl.pallas_call(
        functools.partial(_peer_apply_kernel, tm=tm, ec=ec, unroll=unroll),
        out_shape=jax.ShapeDtypeStruct((n, d), F32),
        grid=(n // tm, ne // ec),
        in_specs=[tok(d),
                  pl.BlockSpec((d, ec), lambda i, c: (0, c)),
                  pl.BlockSpec((ec, d), lambda i, c: (c, 0)),
                  tok(a_idx.shape[1]), tok(b_idx.shape[1]), tok(gw.shape[1]), tok(d)],
        out_specs=tok(d),
        scratch_shapes=[pltpu.VMEM((tm * _W_PITCH, LANES), F32)],
        compiler_params=pltpu.CompilerParams(dimension_semantics=("parallel", "arbitrary"),
                                             vmem_limit_bytes=56 * 1024 * 1024),
        name="peer_apply",
    )(xn_bf16, ut_bf16, v_bf16, a_idx, b_idx, gw, resid)


def _odd_mixer_kernel(h_ref, g_ref, win_ref, cw_ref, cb_ref, wra_ref, bra_ref, wri_ref, bri_ref, lam_ref, wout_ref,
                      o_ref, xbuf_ref, carry_ref, *, tc):
    j = pl.program_id(1)
    d_rnn = cw_ref.shape[1]
    pad = SUBLANES

    @pl.when(j == 0)
    def _():
        xbuf_ref[0:pad, :] = jnp.zeros((pad, d_rnn), F32)
        carry_ref[...] = jnp.zeros(carry_ref.shape, F32)

    h_in = h_ref[...]
    z = _dot(_rmsnorm_rows(h_in, g_ref[...]).astype(BF16), win_ref[...])
    gate = z[:, :d_rnn]
    xbuf_ref[pad:pad + tc, :] = z[:, d_rnn:]
    xc = cb_ref[...] + cw_ref[0:1, :] * xbuf_ref[pl.ds(pad - 3, tc), :]
    for w in range(1, CONV_W):
        xc = xc + cw_ref[w:w + 1, :] * xbuf_ref[pl.ds(pad - 3 + w, tc), :]
    xbuf_ref[0:pad, :] = xbuf_ref[tc:tc + pad, :]

    xcb = xc.astype(BF16)
    nb = d_rnn // RNN_BLOCK_W
    blk = lambda w_ref: jnp.concatenate(
        [_dot(xcb[:, n * RNN_BLOCK_W:(n + 1) * RNN_BLOCK_W], w_ref[n]) for n in range(nb)], axis=1)
    r = _sigmoid(blk(wra_ref) + bra_ref[...])
    gi = _sigmoid(blk(wri_ref) + bri_ref[...])
    nl = -lam_ref[...]
    softplus = jnp.maximum(nl, 0.0) + jnp.log(1.0 + jnp.exp(-jnp.abs(nl)))
    log_a = (-LRU_C) * r * softplus
    a = jnp.exp(log_a)
    u = jnp.sqrt(1.0 - jnp.exp(2.0 * log_a)) * (gi * xc)

    row = lax.broadcasted_iota(jnp.int32, (tc, d_rnn), 0)
    shift = 1
    while shift < tc:
        valid = row >= shift
        a_prev = jnp.where(valid, pltpu.roll(a, shift, 0), 1.0)
        u_prev = jnp.where(valid, pltpu.roll(u, shift, 0), 0.0)
        u = u + a * u_prev
        a = a * a_prev
        shift *= 2
    hs = u + a * carry_ref[...]
    carry_ref[...] = hs[tc - 1:tc, :]

    y = (_gelu(gate) * hs).astype(BF16)
    o_ref[...] = h_in + _dot(y, wout_ref[...])


def odd_mixer_residual(h, norm_g, w_in, conv_w, conv_b, w_ra, b_ra, w_ri, b_ri, lam, w_out, tc=256):
    b, s, d = h.shape
    d_rnn = conv_w.shape[1]
    full = lambda a: pl.BlockSpec(a.shape, lambda bb, j, nd=a.ndim: (0,) * nd)
    row = lambda v: v.astype(F32).reshape(1, -1)
    args = [norm_g.astype(F32).reshape(1, d), w_in.astype(BF16), conv_w.astype(F32), row(conv_b),
            w_ra.astype(BF16), row(b_ra), w_ri.astype(BF16), row(b_ri), row(lam), w_out.astype(BF16)]
    return pl.pallas_call(
        functools.partial(_odd_mixer_kernel, tc=tc),
        out_shape=jax.ShapeDtypeStruct((b, s, d), F32),
        grid=(b, s // tc),
        in_specs=[pl.BlockSpec((None, tc, d), lambda bb, j: (bb, j, 0))] + [full(a) for a in args],
        out_specs=pl.BlockSpec((None, tc, d), lambda bb, j: (bb, j, 0)),
        scratch_shapes=[pltpu.VMEM((tc + 2 * SUBLANES, d_rnn), F32), pltpu.VMEM((1, d_rnn), F32)],
        compiler_params=_cparams("parallel", "arbitrary"),
        name="odd_mixer",
    )(h, *args)


def _rmsnorm_kernel(x_ref, g_ref, o_ref):
    o_ref[...] = _rmsnorm_rows(x_ref[...], g_ref[...])


def rmsnorm_rows(x2d, g, tm=512):
    n, d = x2d.shape
    return pl.pallas_call(
        _rmsnorm_kernel,
        out_shape=jax.ShapeDtypeStruct((n, d), F32),
        grid=(n // tm,),
        in_specs=[pl.BlockSpec((tm, d), lambda i: (i, 0)), pl.BlockSpec((1, d), lambda i: (0, 0))],
        out_specs=pl.BlockSpec((tm, d), lambda i: (i, 0)),
        compiler_params=_cparams("parallel"),
        name="final_rmsnorm",
    )(x2d, g.astype(F32).reshape(1, d))


def even_mixer_residual(h, norm_g, w_in, b_f, cmp_pos, w_cmp_k, w_cmp_v, w_out):
    b, s, d = h.shape
    h2d = h.reshape(b * s, d)
    w_ext, wt_ext = _even_w_ext(w_in)
    outs = norm_proj(h2d, norm_g.astype(F32), w_ext, _EVEN_SEGS, _EVEN_DTYPES, wt_ext, _EVEN_TSEGS, name="even_in_proj")
    fq, fk, nq, ncmp, nk, small = [o.reshape(b, s, -1) for o in outs[:6]]
    fvt, svt, wvt = outs[6:]
    c = fox_gate_cumsum(small[..., :N_FOX_HEADS].transpose(0, 2, 1), b_f)
    o_fox = fox_attention(fq, fk, fvt, c)
    kc, vct = nsa_compress(ncmp, cmp_pos, w_cmp_k, w_cmp_v)
    gates_t = small[..., N_FOX_HEADS:N_FOX_HEADS + NSA_GATES].reshape(b, s, N_NSA_KV, 3 * NSA_GROUP).transpose(0, 2, 3, 1)
    gates_t = jnp.pad(gates_t, ((0, 0), (0, 0), (0, 2 * SUBLANES - 3 * NSA_GROUP), (0, 0)))
    o_nsa = nsa_attention(nq, kc, vct, nk, svt, wvt, gates_t)
    nf = N_FOX_HEADS * HEAD_DIM
    w_n = w_out[nf:].reshape(N_NSA_HEADS, HEAD_DIM, d)
    half_n = (np.arange(N_NSA_HEADS) // NSA_GROUP)[:, None, None, None] == np.arange(2)[None, :, None, None]
    w_n = jnp.where(half_n, w_n[:, None], 0.0).reshape(N_NSA_HEADS * LANES, d)
    pairs = [(o_fox.reshape(b * s, -1), w_out[:nf].astype(BF16)), (o_nsa.reshape(b * s, -1), w_n.astype(BF16))]
    return proj_residual(pairs, h2d, name="even_out_proj").reshape(b, s, d)


def peer_ffn_residual(h2d, norm_g, w_q, sub_keys, u_tab, v_tab):
    d = h2d.shape[1]
    nq = w_q.shape[1]
    q, xn = norm_proj(h2d, norm_g, w_q.astype(BF16), ((0, nq, 1.0), (None, d, 1.0)), (BF16, BF16), name="peer_query")
    a_idx, b_idx, gw = peer_topk(q, sub_keys)
    return peer_apply(xn, u_tab.astype(BF16).T, v_tab.astype(BF16), a_idx, b_idx, gw, h2d)


def kernel(x, even_norm_g, even_w_in, even_b_f, even_cmp_pos, even_w_cmp_k, even_w_cmp_v, even_w_out, odd_norm_g, odd_w_in, odd_conv_w, odd_conv_b, odd_w_ra, odd_b_ra, odd_w_ri, odd_b_ri, odd_lam, odd_w_out, ffn_norm_g, peer_w_q, peer_sub_keys, peer_u, peer_v, final_g):
    b, s, d = x.shape
    depth = ffn_norm_g.shape[0]
    h = x
    for layer in range(depth):
        j = layer // 2
        if layer % 2 == 0:
            h = even_mixer_residual(h, even_norm_g[j], even_w_in[j], even_b_f[j], even_cmp_pos[j],
                                    even_w_cmp_k[j], even_w_cmp_v[j], even_w_out[j])
        else:
            h = odd_mixer_residual(h, odd_norm_g[j], odd_w_in[j], odd_conv_w[j], odd_conv_b[j], odd_w_ra[j],
                                   odd_b_ra[j], odd_w_ri[j], odd_b_ri[j], odd_lam[j], odd_w_out[j])
        h = peer_ffn_residual(h.reshape(b * s, d), ffn_norm_g[layer].astype(F32), peer_w_q[layer],
                              peer_sub_keys[layer], peer_u[layer], peer_v[layer]).reshape(b, s, d)
    return rmsnorm_rows(h.reshape(b * s, d), final_g).reshape(b, s, d)
```

```python
import functools
import math

import jax
import jax.numpy as jnp
import numpy as np
from jax import lax
from jax.experimental import pallas as pl
from jax.experimental.pallas import tpu as pltpu

HEAD_DIM = 64
N_FOX_HEADS = 8
N_NSA_HEADS = 8
N_NSA_KV = 2
NSA_GROUP = N_NSA_HEADS // N_NSA_KV
CMP_LEN = 32
CMP_STRIDE = 16
SLC_LEN = 64
SLC_TOPK = 16
WINDOW = 512
FORCE_BONUS = 1.0e4
NEG = -1.0e30
D_RNN = 1280
RNN_BLOCKS = 10
RNN_BLOCK_W = D_RNN // RNN_BLOCKS
CONV_W = 4
LRU_C = 8.0
PEER_HEADS = 8
PEER_NKEYS = 128
PEER_HALF = 128
PEER_TOPK = 16
RMS_EPS = 1e-6

LANES = 128
SUBLANES = 8
VMEM_LIMIT = 48 * 1024 * 1024

F32 = jnp.float32
BF16 = jnp.bfloat16


def _cparams(*sem):
    return pltpu.CompilerParams(dimension_semantics=sem, vmem_limit_bytes=VMEM_LIMIT)


def _dot(a, b):
    return lax.dot_general(a, b, (((1,), (0,)), ((), ())), preferred_element_type=F32)


def _dot_nt(a, b):
    return lax.dot_general(a, b, (((1,), (1,)), ((), ())), preferred_element_type=F32)


def _rmsnorm_rows(x, g):
    return x * lax.rsqrt(jnp.mean(x * x, axis=-1, keepdims=True) + RMS_EPS) * g


def _gelu(x):
    return 0.5 * x * (1.0 + lax.erf(x * (1.0 / math.sqrt(2.0))))


def _sigmoid(x):
    return 1.0 / (1.0 + jnp.exp(-x))


def _norm_proj_kernel(x_ref, g_ref, w_ref, wt_ref, *out_refs, segs, tsegs):
    xn = _rmsnorm_rows(x_ref[...], g_ref[...]).astype(BF16)
    for o_ref, (start, width, scale) in zip(out_refs, segs):
        if start is None:
            o_ref[...] = xn
            continue
        y = _dot(xn, w_ref[:, start:start + width])
        if scale != 1.0:
            y = y * scale
        o_ref[...] = y.astype(o_ref.dtype)
    for o_ref, (start, width) in zip(out_refs[len(segs):], tsegs):
        o_ref[...] = _dot_nt(wt_ref[start:start + width, :], xn).astype(o_ref.dtype)


def norm_proj(x2d, g, w_bf16, segs, dtypes, wt_bf16=None, tsegs=(), tm=256, name="norm_proj"):
    n, d = x2d.shape
    if wt_bf16 is None:
        wt_bf16 = jnp.zeros((SUBLANES, d), BF16)
    outs = [jax.ShapeDtypeStruct((n, wd), dt) for (_, wd, _), dt in zip(segs, dtypes)]
    outs += [jax.ShapeDtypeStruct((wd, n), BF16) for (_, wd) in tsegs]
    return pl.pallas_call(
        functools.partial(_norm_proj_kernel, segs=tuple(segs), tsegs=tuple(tsegs)),
        out_shape=outs,
        grid=(n // tm,),
        in_specs=[pl.BlockSpec((tm, d), lambda i: (i, 0)),
                  pl.BlockSpec((1, d), lambda i: (0, 0)),
                  pl.BlockSpec(w_bf16.shape, lambda i: (0, 0)),
                  pl.BlockSpec(wt_bf16.shape, lambda i: (0, 0))],
        out_specs=([pl.BlockSpec((tm, wd), lambda i: (i, 0)) for (_, wd, _) in segs]
                   + [pl.BlockSpec((wd, tm), lambda i: (0, i)) for (_, wd) in tsegs]),
        compiler_params=_cparams("parallel"),
        name=name,
    )(x2d, g.reshape(1, d), w_bf16, wt_bf16)


def _proj_residual_kernel(*refs):
    r_ref, o_ref = refs[-2], refs[-1]
    acc = r_ref[...]
    for a_ref, w_ref in zip(refs[0:-2:2], refs[1:-2:2]):
        acc = acc + _dot(a_ref[...], w_ref[...])
    o_ref[...] = acc


def proj_residual(pairs, resid, tm=512, name="proj_residual"):
    n, d = resid.shape
    args, specs = [], []
    for a, w in pairs:
        args += [a, w]
        specs += [pl.BlockSpec((tm, a.shape[1]), lambda i: (i, 0)), pl.BlockSpec(w.shape, lambda i: (0, 0))]
    return pl.pallas_call(
        _proj_residual_kernel,
        out_shape=jax.ShapeDtypeStruct((n, d), F32),
        grid=(n // tm,),
        in_specs=specs + [pl.BlockSpec((tm, d), lambda i: (i, 0))],
        out_specs=pl.BlockSpec((tm, d), lambda i: (i, 0)),
        compiler_params=_cparams("parallel"),
        name=name,
    )(*args, resid)


def _fox_gate_kernel(f_ref, b_ref, c_ref):
    z = f_ref[...] + b_ref[...]
    x = jnp.minimum(z, 0.0) - jnp.log(1.0 + jnp.exp(-jnp.abs(z)))
    s = x.shape[-1]
    lane = lax.broadcasted_iota(jnp.int32, x.shape, 1)
    shift = 1
    while shift < s:
        x = x + jnp.where(lane >= shift, pltpu.roll(x, shift, 1), 0.0)
        shift *= 2
    c_ref[...] = x


def fox_gate_cumsum(f_t, b_f):
    b, h, s = f_t.shape
    return pl.pallas_call(
        _fox_gate_kernel,
        out_shape=jax.ShapeDtypeStruct((b, h, s), F32),
        grid=(b,),
        in_specs=[pl.BlockSpec((None, h, s), lambda i: (i, 0, 0)),
                  pl.BlockSpec((h, 1), lambda i: (0, 0))],
        out_specs=pl.BlockSpec((None, h, s), lambda i: (i, 0, 0)),
        compiler_params=_cparams("parallel"),
        name="fox_gate_cumsum",
    )(f_t, b_f.reshape(h, 1).astype(F32))


ATTN_KB = 512


def _fox_attn_kernel(q_ref, k_ref, vt_ref, ct_ref, cs_ref, o_ref, qs_ref, m_ref, l_ref, acc_ref, *, tq):
    KB = ATTN_KB
    i = pl.program_id(2)
    t0 = i * tq
    for hh in range(2):
        qs_ref[hh * tq:(hh + 1) * tq, :] = q_ref[:, hh * LANES:(hh + 1) * LANES]
    qs = qs_ref[...]
    m_ref[...] = jnp.full(m_ref.shape, -jnp.inf, F32)
    l_ref[...] = jnp.zeros(l_ref.shape, F32)
    acc_ref[...] = jnp.zeros(acc_ref.shape, F32)
    t_pos = t0 + lax.broadcasted_iota(jnp.int32, (KB, tq), 1)
    s_in = lax.broadcasted_iota(jnp.int32, (KB, tq), 0)

    def step(kb, masked):
        start = pl.multiple_of(kb * KB, KB)
        st = _dot_nt(k_ref[pl.ds(start, KB), :], qs)
        vt = vt_ref[:, pl.ds(start, KB)]
        cs = cs_ref[pl.ds(start, KB), :]
        for hh in range(2):
            s = st[:, hh * tq:(hh + 1) * tq] + (ct_ref[hh:hh + 1, :] - cs[:, hh:hh + 1])
            if masked:
                s = jnp.where(start + s_in <= t_pos, s, NEG)
            m_old = m_ref[hh:hh + 1, :]
            m_new = jnp.maximum(m_old, jnp.max(s, axis=0, keepdims=True))
            alpha = jnp.exp(m_old - m_new)
            p = jnp.exp(s - m_new)
            l_ref[hh:hh + 1, :] = alpha * l_ref[hh:hh + 1, :] + jnp.sum(p, axis=0, keepdims=True)
            acc_ref[hh] = alpha * acc_ref[hh] + _dot(vt, p.astype(BF16))
            m_ref[hh:hh + 1, :] = m_new

    n_full = t0 // KB

    def full_body(kb, carry):
        step(kb, False)
        return carry

    lax.fori_loop(0, n_full, full_body, 0)
    step(n_full, True)

    row = lax.broadcasted_iota(jnp.int32, (LANES, tq), 0)
    o0 = acc_ref[0] / l_ref[0:1, :]
    o1 = acc_ref[1] / l_ref[1:2, :]
    ot = jnp.where(row < HEAD_DIM, o0, o1)
    for blk in range(tq // LANES):
        o_ref[blk * LANES:(blk + 1) * LANES, :] = ot[:, blk * LANES:(blk + 1) * LANES].T.astype(o_ref.dtype)


def fox_attention(fq, fk, fvt, c, tq=256):
    b, s, _ = fk.shape
    hp = N_FOX_HEADS // 2
    ct = c.reshape(b, hp, 2, s)
    cs = ct.transpose(0, 1, 3, 2)
    return pl.pallas_call(
        functools.partial(_fox_attn_kernel, tq=tq),
        out_shape=jax.ShapeDtypeStruct((b, s, N_FOX_HEADS * HEAD_DIM), BF16),
        grid=(b, hp, s // tq),
        in_specs=[pl.BlockSpec((None, tq, 2 * LANES), lambda bb, p, i: (bb, i, p)),
                  pl.BlockSpec((None, s, LANES), lambda bb, p, i: (bb, 0, p)),
                  pl.BlockSpec((LANES, s), lambda bb, p, i: (p, bb)),
                  pl.BlockSpec((None, None, 2, tq), lambda bb, p, i: (bb, p, 0, i)),
                  pl.BlockSpec((None, None, s, 2), lambda bb, p, i: (bb, p, 0, 0))],
        out_specs=pl.BlockSpec((None, tq, LANES), lambda bb, p, i: (bb, i, p)),
        scratch_shapes=[pltpu.VMEM((2 * tq, LANES), BF16), pltpu.VMEM((2, tq), F32), pltpu.VMEM((2, tq), F32),
                        pltpu.VMEM((2, LANES, tq), F32)],
        compiler_params=_cparams("parallel", "parallel", "arbitrary"),
        name="fox_attention",
    )(fq, fk, fvt, ct, cs)


FOX_QKV = 3 * N_FOX_HEADS * HEAD_DIM
NSA_Q = N_NSA_HEADS * HEAD_DIM
NSA_KV = 6 * N_NSA_KV * HEAD_DIM
NSA_GATES = 3 * N_NSA_HEADS
_QK_SCALE = HEAD_DIM ** -0.5

_EVEN_SEGS = ((0, 1024, _QK_SCALE), (1024, 512, 1.0), (1536, 1024, _QK_SCALE),
              (2560, 256, 1.0), (2816, 256, 1.0), (3072, 128, 1.0))
_EVEN_DTYPES = (BF16, BF16, BF16, F32, BF16, F32)
_EVEN_TSEGS = ((0, 512), (512, 128), (640, 128))


def _even_w_ext(w_in):
    d = w_in.shape[0]
    o = 0
    w_fq = w_in[:, o:o + 512].reshape(d, N_FOX_HEADS, HEAD_DIM); o += 512
    w_fk = w_in[:, o:o + 512]; o += 512
    w_fv = w_in[:, o:o + 512]; o += 512
    w_ff = w_in[:, o:o + N_FOX_HEADS]; o += N_FOX_HEADS
    w_nq = w_in[:, o:o + NSA_Q].reshape(d, N_NSA_HEADS, HEAD_DIM); o += NSA_Q
    w_nkv = w_in[:, o:o + NSA_KV]; o += NSA_KV
    w_ng = w_in[:, o:o + NSA_GATES]
    half_f = (np.arange(N_FOX_HEADS) % 2)[None, :, None, None] == np.arange(2)[None, None, :, None]
    fq = jnp.where(half_f, w_fq[:, :, None, :], 0.0).reshape(d, N_FOX_HEADS * LANES)
    half_n = (np.arange(N_NSA_HEADS) // NSA_GROUP)[None, :, None, None] == np.arange(2)[None, None, :, None]
    nq = jnp.where(half_n, w_nq[:, :, None, :], 0.0).reshape(d, N_NSA_HEADS * LANES)
    small = jnp.concatenate([w_ff, w_ng, jnp.zeros((d, LANES - N_FOX_HEADS - NSA_GATES), w_in.dtype)], axis=1)
    w = jnp.concatenate([fq, w_fk, nq, w_nkv[:, 0:256], w_nkv[:, 256:384], w_nkv[:, 512:640], small], axis=1)
    wt = jnp.concatenate([w_fv, w_nkv[:, 384:512], w_nkv[:, 640:768]], axis=1).T
    return w.astype(BF16), wt.astype(BF16)


def _nsa_compress_kernel(x_ref, pos_ref, w_ref, kc_ref, vct_ref):
    n = x_ref.shape[2]
    for kind in range(2):
        acc = jnp.zeros((n, LANES), F32)
        for g in range(N_NSA_KV):
            x = x_ref[kind, g]
            top = _dot((x + pos_ref[0:1, :]).astype(BF16), w_ref[kind, g, 0])
            bot = _dot((x + pos_ref[1:2, :]).astype(BF16), w_ref[kind, g, 1])
            acc = acc + top + pltpu.roll(bot, n - 1, 0)
        if kind == 0:
            kc_ref[...] = acc.astype(kc_ref.dtype)
        else:
            for blk in range(n // LANES):
                sl = slice(blk * LANES, (blk + 1) * LANES)
                vct_ref[:, sl] = acc[sl, :].T.astype(vct_ref.dtype)


def nsa_compress(ncmp, cmp_pos, w_cmp_k, w_cmp_v):
    b, s, _ = ncmp.shape
    n = s // CMP_STRIDE
    half = CMP_STRIDE * HEAD_DIM
    x = ncmp.reshape(b, n, CMP_STRIDE, 2, N_NSA_KV, HEAD_DIM).transpose(0, 3, 4, 1, 2, 5).reshape(b, 2, N_NSA_KV, n, half)
    pos = cmp_pos.astype(F32).reshape(2, half)
    w = jnp.stack([w_cmp_k, w_cmp_v]).astype(F32).reshape(2, 2, half, HEAD_DIM)
    place = (np.arange(N_NSA_KV)[:, None, None] == np.arange(2)[None, :, None])
    w_pad = jnp.where(place[None, :, None, None], w[:, None, :, :, None, :], 0.0)
    w_pad = w_pad.reshape(2, N_NSA_KV, 2, half, LANES).astype(BF16)
    return pl.pallas_call(
        _nsa_compress_kernel,
        out_shape=[jax.ShapeDtypeStruct((b, n, LANES), BF16), jax.ShapeDtypeStruct((b, LANES, n), BF16)],
        grid=(b,),
        in_specs=[pl.BlockSpec((None, 2, N_NSA_KV, n, half), lambda i: (i, 0, 0, 0, 0)),
                  pl.BlockSpec((2, half), lambda i: (0, 0)),
                  pl.BlockSpec((2, N_NSA_KV, 2, half, LANES), lambda i: (0, 0, 0, 0, 0))],
        out_specs=[pl.BlockSpec((None, n, LANES), lambda i: (i, 0, 0)),
                   pl.BlockSpec((None, LANES, n), lambda i: (i, 0, 0))],
        compiler_params=_cparams("parallel"),
        name="nsa_compress",
    )(x, pos, w_pad)


def _nsa_attn_kernel(q_ref, kc_ref, vct_ref, nk_ref, svt_ref, wvt_ref, gate_ref, slope_ref, ovt_ref, o_ref,
                     q4_ref, m_ref, l_ref, acc_ref, selt_ref, *, tq):
    g = pl.program_id(1)
    i = pl.program_id(2)
    t0 = i * tq
    n_cmp = kc_ref.shape[0]
    R = NSA_GROUP

    for r in range(R):
        q4_ref[r * tq:(r + 1) * tq, :] = q_ref[:, r * LANES:(r + 1) * LANES]
    q4 = q4_ref[...]

    t_c = t0 + lax.broadcasted_iota(jnp.int32, (n_cmp, tq), 1)
    cend = CMP_STRIDE * lax.broadcasted_iota(jnp.int32, (n_cmp, tq), 0) + (CMP_LEN - 1)
    cmask = cend <= t_c
    cdist = (t_c - cend).astype(F32)
    any_c = (t_c[0:1, :] >= CMP_LEN - 1).astype(F32)
    s_c = _dot_nt(kc_ref[...], q4)
    imp_t = jnp.zeros((ovt_ref.shape[0], tq), F32)
    o_cmp = []
    for r in range(R):
        s = s_c[:, r * tq:(r + 1) * tq] - slope_ref[r:r + 1, 0:1] * cdist
        s = jnp.where(cmask, s, NEG)
        e = jnp.exp(s - jnp.max(s, axis=0, keepdims=True))
        p = (e / jnp.sum(e, axis=0, keepdims=True)) * any_c
        pb = p.astype(BF16)
        o_cmp.append(_dot(vct_ref[...], pb))
        imp_t = imp_t + _dot(ovt_ref[...], pb)

    n_sel = ovt_ref.shape[0]
    jrow = lax.broadcasted_iota(jnp.int32, (n_sel, tq), 0)
    cur = (t0 + lax.broadcasted_iota(jnp.int32, (n_sel, tq), 1)) // SLC_LEN
    forced = ((jrow == 0) | (jrow == cur) | (jrow == cur - 1)).astype(F32)
    imp_t = jnp.where(jrow <= cur, imp_t + FORCE_BONUS * forced, -jnp.inf)
    ngrp = n_sel // SUBLANES
    grp = [imp_t[gi * SUBLANES:(gi + 1) * SUBLANES, :] for gi in range(ngrp)]
    sub8 = lax.broadcasted_iota(jnp.int32, (SUBLANES, tq), 0)
    rank = [jnp.zeros((SUBLANES, tq), jnp.int32) for _ in range(ngrp)]
    for jp in range(n_sel):
        other = imp_t[jp:jp + 1, :]
        for gi in range(ngrp):
            lo = gi * SUBLANES
            if lo + SUBLANES - 1 < jp:
                ahead = other > grp[gi]
            elif lo > jp:
                ahead = other >= grp[gi]
            else:
                ahead = (other > grp[gi]) | ((other == grp[gi]) & (sub8 > jp - lo))
            rank[gi] = rank[gi] + ahead.astype(jnp.int32)
    for gi in range(ngrp):
        selt_ref[gi * SUBLANES:(gi + 1) * SUBLANES, :] = (rank[gi] < SLC_TOPK).astype(F32)

    KB = ATTN_KB
    t_pos = t0 + lax.broadcasted_iota(jnp.int32, (KB, tq), 1)
    s_in = lax.broadcasted_iota(jnp.int32, (KB, tq), 0)
    per_kb = KB // SLC_LEN
    m_ref[...] = jnp.full(m_ref.shape, -jnp.inf, F32)
    l_ref[...] = jnp.zeros(l_ref.shape, F32)
    acc_ref[...] = jnp.zeros(acc_ref.shape, F32)

    def sel_body(kb, carry):
        start = pl.multiple_of(kb * KB, KB)
        dist_i = t_pos - (start + s_in)
        dist = dist_i.astype(F32)
        chosen = jnp.concatenate(
            [jnp.broadcast_to(selt_ref[pl.ds(kb * per_kb + c, 1), :], (SLC_LEN, tq)) for c in range(per_kb)], axis=0)
        mask = (dist_i >= 0) & (chosen > 0.5)
        st = _dot_nt(nk_ref[pl.ds(start, KB), 0:LANES], q4)
        vt = svt_ref[:, pl.ds(start, KB)]
        for r in range(R):
            s = st[:, r * tq:(r + 1) * tq] - slope_ref[r:r + 1, 0:1] * dist
            s = jnp.where(mask, s, NEG)
            m_old = m_ref[r:r + 1, :]
            m_new = jnp.maximum(m_old, jnp.max(s, axis=0, keepdims=True))
            alpha = jnp.exp(m_old - m_new)
            p = jnp.exp(s - m_new)
            l_ref[r:r + 1, :] = alpha * l_ref[r:r + 1, :] + jnp.sum(p, axis=0, keepdims=True)
            acc_ref[r] = alpha * acc_ref[r] + _dot(vt, p.astype(BF16))
            m_ref[r:r + 1, :] = m_new
        return carry

    lax.fori_loop(0, t0 // KB + 1, sel_body, 0)

    nwb = (WINDOW + tq) // tq
    w_first = i - (nwb - 1)
    starts = [pl.multiple_of(jnp.maximum(w_first + d, 0) * tq, tq) for d in range(nwb)]
    kw = jnp.concatenate([nk_ref[pl.ds(st_, tq), LANES:2 * LANES] for st_ in starts], axis=0)
    vtw = jnp.concatenate([wvt_ref[:, pl.ds(st_, tq)] for st_ in starts], axis=1)
    wl = nwb * tq
    key_pos = w_first * tq + lax.broadcasted_iota(jnp.int32, (wl, tq), 0)
    wdist_i = (t0 + lax.broadcasted_iota(jnp.int32, (wl, tq), 1)) - key_pos
    wmask = (wdist_i >= 0) & (wdist_i < WINDOW) & (key_pos >= 0)
    wdist = wdist_i.astype(F32)
    stw = _dot_nt(kw, q4)
    o_win = []
    for r in range(R):
        s = jnp.where(wmask, stw[:, r * tq:(r + 1) * tq] - slope_ref[r:r + 1, 0:1] * wdist, NEG)
        e = jnp.exp(s - jnp.max(s, axis=0, keepdims=True))
        p = e / jnp.sum(e, axis=0, keepdims=True)
        o_win.append(_dot(vtw, p.astype(BF16)))

    gates = _sigmoid(gate_ref[...])
    row = lax.broadcasted_iota(jnp.int32, (LANES, tq), 0)
    mine = (row // HEAD_DIM) == g
    for r in range(R):
        ot = (gates[3 * r:3 * r + 1, :] * o_cmp[r] + gates[3 * r + 1:3 * r + 2, :] * (acc_ref[r] / l_ref[r:r + 1, :])
              + gates[3 * r + 2:3 * r + 3, :] * o_win[r])
        ot = jnp.where(mine, ot, 0.0)
        for blk in range(tq // LANES):
            o_ref[blk * LANES:(blk + 1) * LANES, r * LANES:(r + 1) * LANES] = (
                ot[:, blk * LANES:(blk + 1) * LANES].T.astype(o_ref.dtype))


def _alibi_slopes(n):
    return np.asarray(2.0 ** (-8.0 * np.arange(1, n + 1) / n), np.float32)


def nsa_attention(nq, kc, vct, nk, svt, wvt, gates_t, tq=128):
    b, s, _ = nk.shape
    n_cmp = s // CMP_STRIDE
    n_sel = s // SLC_LEN
    slopes = np.broadcast_to(_alibi_slopes(N_NSA_HEADS).reshape(N_NSA_KV, NSA_GROUP, 1), (N_NSA_KV, NSA_GROUP, LANES))
    slopes = jnp.asarray(np.concatenate([slopes, np.zeros((N_NSA_KV, SUBLANES - NSA_GROUP, LANES), np.float32)], axis=1))
    cstart = CMP_STRIDE * np.arange(n_cmp)
    sstart = SLC_LEN * np.arange(n_sel)
    ov = ((cstart[None, :] < sstart[:, None] + SLC_LEN) & (cstart[None, :] + CMP_LEN > sstart[:, None])
          & (cstart[None, :] + CMP_LEN <= s))
    ovt = jnp.asarray(ov, BF16)
    return pl.pallas_call(
        functools.partial(_nsa_attn_kernel, tq=tq),
        out_shape=jax.ShapeDtypeStruct((b, s, N_NSA_HEADS * LANES), BF16),
        grid=(b, N_NSA_KV, s // tq),
        in_specs=[pl.BlockSpec((None, tq, NSA_GROUP * LANES), lambda bb, g, i: (bb, i, g)),
                  pl.BlockSpec((None, n_cmp, LANES), lambda bb, g, i: (bb, 0, 0)),
                  pl.BlockSpec((None, LANES, n_cmp), lambda bb, g, i: (bb, 0, 0)),
                  pl.BlockSpec((None, s, 2 * LANES), lambda bb, g, i: (bb, 0, 0)),
                  pl.BlockSpec((LANES, s), lambda bb, g, i: (0, bb)),
                  pl.BlockSpec((LANES, s), lambda bb, g, i: (0, bb)),
                  pl.BlockSpec((None, None, 2 * SUBLANES, tq), lambda bb, g, i: (bb, g, 0, i)),
                  pl.BlockSpec((None, SUBLANES, LANES), lambda bb, g, i: (g, 0, 0)),
                  pl.BlockSpec((n_sel, n_cmp), lambda bb, g, i: (0, 0))],
        out_specs=pl.BlockSpec((None, tq, NSA_GROUP * LANES), lambda bb, g, i: (bb, i, g)),
        scratch_shapes=[pltpu.VMEM((NSA_GROUP * tq, LANES), BF16),
                        pltpu.VMEM((NSA_GROUP, tq), F32), pltpu.VMEM((NSA_GROUP, tq), F32),
                        pltpu.VMEM((NSA_GROUP, LANES, tq), F32), pltpu.VMEM((n_sel, tq), F32)],
        compiler_params=_cparams("parallel", "parallel", "arbitrary"),
        name="nsa_attention",
    )(nq, kc, vct, nk, svt, wvt, gates_t, slopes, ovt)


def _peer_cand_layout():
    K = PEER_TOPK
    groups = [((0, 1), (0, K))]
    groups += [((j1, j1 + 1), (0, SUBLANES)) for j1 in range(1, SUBLANES)]
    groups += [((SUBLANES, K), (0, 1))]
    flat = []
    for (a0, a1), (b0, b1) in groups:
        flat += [j1 * K + j2 for j1 in range(a0, a1) for j2 in range(b0, b1)]
    return groups, np.asarray(flat, np.int32)


def _extract_top(vals, tags, n, write):
    big = jnp.int32(2 ** 30)
    for it in range(n):
        m = jnp.max(vals, axis=0, keepdims=True)
        tag = jnp.min(jnp.where(vals == m, tags, big), axis=0, keepdims=True)
        write(it, m, tag)
        vals = jnp.where(tags == tag, -jnp.inf, vals)


def _oddeven_merge_sort_pairs(n):
    pairs = []

    def merge(lo, hi, r):
        step = r * 2
        if step < hi - lo:
            merge(lo, hi, step)
            merge(lo + r, hi, step)
            pairs.extend((i, i + r) for i in range(lo + r, hi - r, step))
        else:
            pairs.append((lo, lo + r))

    def sort(lo, hi):
        if hi - lo >= 1:
            mid = lo + (hi - lo) // 2
            sort(lo, mid)
            sort(mid + 1, hi)
            merge(lo, hi, 1)

    sort(0, n - 1)
    return pairs


def _compare_exchange(v, ix, a, b):
    first = v[a] >= v[b]
    v[a], v[b] = jnp.maximum(v[a], v[b]), jnp.minimum(v[a], v[b])
    ix[a], ix[b] = jnp.where(first, ix[a], ix[b]), jnp.where(first, ix[b], ix[a])


def _sorted_top16_network(scores):
    n = PEER_TOPK
    t = scores.shape[1]
    orig = [scores[r * SUBLANES:(r + 1) * SUBLANES, :] for r in range(n)]
    v = list(orig)
    sub = lax.broadcasted_iota(jnp.int32, (SUBLANES, t), 0)
    ix = [sub + r * SUBLANES for r in range(n)]
    for a, b in _oddeven_merge_sort_pairs(n):
        _compare_exchange(v, ix, a, b)
    shift = SUBLANES // 2
    while shift >= 1:
        bv = [pltpu.roll(x, shift, 0) for x in v]
        bi = [pltpu.roll(x, shift, 0) for x in ix]
        for r in range(n):
            o = n - 1 - r
            first = v[r] >= bv[o]
            ix[r] = jnp.where(first, ix[r], bi[o])
            v[r] = jnp.maximum(v[r], bv[o])
        d = n // 2
        while d >= 1:
            for r in range(n):
                if r & d == 0:
                    _compare_exchange(v, ix, r, r + d)
            d //= 2
        shift //= 2
    strict = v[0] > v[1]
    for r in range(1, n - 1):
        strict = strict & (v[r] > v[r + 1])
    at_least = jnp.zeros((SUBLANES, t), jnp.int32)
    for r in range(n):
        at_least = at_least + (orig[r] >= v[n - 1]).astype(jnp.int32)
    sure = strict[0:1, :] & (jnp.sum(at_least, axis=0, keepdims=True) == n)
    return [x[0:1, :] for x in v], [x[0:1, :] for x in ix], sure


def _peer_topk_kernel(q_ref, keys_ref, flat_ref, a_ref, b_ref, g_ref,
                      s1_ref, s2_ref, i1_ref, i2_ref, bs_ref, bid_ref, at_ref, bt_ref, gt_ref, *, tm, groups):
    K = PEER_TOPK
    kidx = lax.broadcasted_iota(jnp.int32, (PEER_NKEYS, tm), 0)
    flat = flat_ref[...]
    for h in range(PEER_HEADS):
        for c, (s_ref, i_ref) in enumerate(((s1_ref, i1_ref), (s2_ref, i2_ref))):
            blk = (2 * h + c) * LANES
            scores = _dot_nt(keys_ref[h, c], q_ref[:, blk:blk + LANES])

            def write(it, m, tag, s_ref=s_ref, i_ref=i_ref):
                s_ref[it:it + 1, :] = m
                i_ref[it:it + 1, :] = tag

            vals, idxs, sure = _sorted_top16_network(scores)
            for it in range(K):
                write(it, vals[it], idxs[it])
            unsure = jnp.max(jnp.where(sure, 0, 1))

            @pl.when(unsure > 0)
            def _(scores=scores, write=write):
                _extract_top(scores, kidx, K, write)
        s1, s2, i1, i2 = s1_ref[...], s2_ref[...], i1_ref[...], i2_ref[...]
        cand, ids = [], []
        for (a0, a1), (b0, b1) in groups:
            cand.append(s1[a0:a1] + s2[b0:b1])
            ids.append(i1[a0:a1] * PEER_NKEYS + i2[b0:b1])
        cand = jnp.concatenate(cand, axis=0)
        tags = flat * (PEER_NKEYS * PEER_NKEYS) + jnp.concatenate(ids, axis=0)

        def write2(it, m, tag):
            bs_ref[it:it + 1, :] = m
            bid_ref[it:it + 1, :] = tag

        _extract_top(cand, tags, K, write2)
        bs = bs_ref[...]
        e = jnp.exp(bs - jnp.max(bs, axis=0, keepdims=True))
        gt_ref[h * K:(h + 1) * K, :] = e / jnp.sum(e, axis=0, keepdims=True)
        bid = bid_ref[...]
        at_ref[h * K:(h + 1) * K, :] = (bid // PEER_NKEYS) % PEER_NKEYS
        bt_ref[h * K:(h + 1) * K, :] = bid % PEER_NKEYS
    for blk in range(tm // LANES):
        sl = slice(blk * LANES, (blk + 1) * LANES)
        a_ref[sl, :] = at_ref[:, sl].T
        b_ref[sl, :] = bt_ref[:, sl].T
        g_ref[sl, :] = gt_ref[:, sl].T


def peer_topk(q_bf16, sub_keys, tm=256):
    n = q_bf16.shape[0]
    groups, flat = _peer_cand_layout()
    nc = flat.shape[0]
    flat_b = jnp.asarray(np.broadcast_to(flat[:, None], (nc, tm)))
    hk = PEER_HEADS * PEER_TOPK
    row = lambda dt: jax.ShapeDtypeStruct((n, hk), dt)
    return pl.pallas_call(
        functools.partial(_peer_topk_kernel, tm=tm, groups=groups),
        out_shape=[row(jnp.int32), row(jnp.int32), row(F32)],
        grid=(n // tm,),
        in_specs=[pl.BlockSpec((tm, q_bf16.shape[1]), lambda i: (i, 0)),
                  pl.BlockSpec(sub_keys.shape, lambda i: (0, 0, 0, 0)),
                  pl.BlockSpec((nc, tm), lambda i: (0, 0))],
        out_specs=[pl.BlockSpec((tm, hk), lambda i: (i, 0))] * 3,
        scratch_shapes=[pltpu.VMEM((PEER_TOPK, tm), F32), pltpu.VMEM((PEER_TOPK, tm), F32),
                        pltpu.VMEM((PEER_TOPK, tm), jnp.int32), pltpu.VMEM((PEER_TOPK, tm), jnp.int32),
                        pltpu.VMEM((PEER_TOPK, tm), F32), pltpu.VMEM((PEER_TOPK, tm), jnp.int32),
                        pltpu.VMEM((hk, tm), jnp.int32), pltpu.VMEM((hk, tm), jnp.int32),
                        pltpu.VMEM((hk, tm), F32)],
        compiler_params=_cparams("parallel"),
        name="peer_topk",
    )(q_bf16, sub_keys.astype(BF16), flat_b)


_W_ROWS = PEER_NKEYS
_W_PITCH = _W_ROWS + SUBLANES


def _peer_apply_kernel(x_ref, ut_ref, v_ref, a_ref, b_ref, g_ref, r_ref, o_ref, w_ref, *, tm, ec, unroll):
    c = pl.program_id(1)
    slabs = ec // PEER_NKEYS
    steps = _W_ROWS // slabs
    half = c // steps
    cc = c % steps

    @pl.when(c == 0)
    def _():
        o_ref[...] = r_ref[...]

    @pl.when(cc == 0)
    def _():
        sub_a = half * _W_ROWS + lax.broadcasted_iota(jnp.int32, (_W_ROWS, LANES), 0)
        sub_b = lax.broadcasted_iota(jnp.int32, (PEER_NKEYS, LANES), 0)

        def build(tb, carry):
            for tt in range(unroll):
                t = tb * unroll + tt
                a = a_ref[pl.ds(t, 1), :]
                b = b_ref[pl.ds(t, 1), :]
                g = g_ref[pl.ds(t, 1), :]
                pt = jnp.where(sub_a == a, g, 0.0).astype(BF16)
                qt = jnp.where(sub_b == b, 1.0, 0.0).astype(BF16)
                w_ref[pl.ds(pl.multiple_of(t * _W_PITCH, SUBLANES), _W_ROWS), :] = _dot_nt(pt, qt)
            return carry

        lax.fori_loop(0, tm // unroll, build, 0)

    act = _dot(x_ref[...], ut_ref[...])
    w = jnp.concatenate([w_ref[pl.ds(cc * slabs + s, tm, stride=_W_PITCH), :] for s in range(slabs)], axis=1)
    o_ref[...] += _dot((_gelu(act) * w).astype(BF16), v_ref[...])


def peer_apply(xn_bf16, ut_bf16, v_bf16, a_idx, b_idx, gw, resid, tm=256, ec=2048, unroll=32):
    n, d = xn_bf16.shape
    ne = v_bf16.shape[0]
    assert ne == PEER_NKEYS * PEER_NKEYS and _W_ROWS % (ec // PEER_NKEYS) == 0
    tok = lambda w: pl.BlockSpec((tm, w), lambda i, c: (i, 0))
    return pl.pallas_call(
        functools.partial(_peer_apply_kernel, tm=tm, ec=ec, unroll=unroll),
        out_shape=jax.ShapeDtypeStruct((n, d), F32),
        grid=(n // tm, ne // ec),
        in_specs=[tok(d),
                  pl.BlockSpec((d, ec), lambda i, c: (0, c)),
                  pl.BlockSpec((ec, d), lambda i, c: (c, 0)),
                  tok(a_idx.shape[1]), tok(b_idx.shape[1]), tok(gw.shape[1]), tok(d)],
        out_specs=tok(d),
        scratch_shapes=[pltpu.VMEM((tm * _W_PITCH, LANES), F32)],
        compiler_params=pltpu.CompilerParams(dimension_semantics=("parallel", "arbitrary"),
                                             vmem_limit_bytes=56 * 1024 * 1024),
        name="peer_apply",
    )(xn_bf16, ut_bf16, v_bf16, a_idx, b_idx, gw, resid)


def _odd_mixer_kernel(h_ref, g_ref, win_ref, cw_ref, cb_ref, wra_ref, bra_ref, wri_ref, bri_ref, lam_ref, wout_ref,
                      o_ref, xbuf_ref, carry_ref, *, tc):
    j = pl.program_id(1)
    d_rnn = cw_ref.shape[1]
    pad = SUBLANES

    @pl.when(j == 0)
    def _():
        xbuf_ref[0:pad, :] = jnp.zeros((pad, d_rnn), F32)
        carry_ref[...] = jnp.zeros(carry_ref.shape, F32)

    h_in = h_ref[...]
    z = _dot(_rmsnorm_rows(h_in, g_ref[...]).astype(BF16), win_ref[...])
    gate = z[:, :d_rnn]
    xbuf_ref[pad:pad + tc, :] = z[:, d_rnn:]
    xc = cb_ref[...] + cw_ref[0:1, :] * xbuf_ref[pl.ds(pad - 3, tc), :]
    for w in range(1, CONV_W):
        xc = xc + cw_ref[w:w + 1, :] * xbuf_ref[pl.ds(pad - 3 + w, tc), :]
    xbuf_ref[0:pad, :] = xbuf_ref[tc:tc + pad, :]

    xcb = xc.astype(BF16)
    nb = d_rnn // RNN_BLOCK_W
    blk = lambda w_ref: jnp.concatenate(
        [_dot(xcb[:, n * RNN_BLOCK_W:(n + 1) * RNN_BLOCK_W], w_ref[n]) for n in range(nb)], axis=1)
    r = _sigmoid(blk(wra_ref) + bra_ref[...])
    gi = _sigmoid(blk(wri_ref) + bri_ref[...])
    nl = -lam_ref[...]
    softplus = jnp.maximum(nl, 0.0) + jnp.log(1.0 + jnp.exp(-jnp.abs(nl)))
    log_a = (-LRU_C) * r * softplus
    a = jnp.exp(log_a)
    u = jnp.sqrt(1.0 - jnp.exp(2.0 * log_a)) * (gi * xc)

    row = lax.broadcasted_iota(jnp.int32, (tc, d_rnn), 0)
    shift = 1
    while shift < tc:
        valid = row >= shift
        a_prev = jnp.where(valid, pltpu.roll(a, shift, 0), 1.0)
        u_prev = jnp.where(valid, pltpu.roll(u, shift, 0), 0.0)
        u = u + a * u_prev
        a = a * a_prev
        shift *= 2
    hs = u + a * carry_ref[...]
    carry_ref[...] = hs[tc - 1:tc, :]

    y = (_gelu(gate) * hs).astype(BF16)
    o_ref[...] = h_in + _dot(y, wout_ref[...])


def odd_mixer_residual(h, norm_g, w_in, conv_w, conv_b, w_ra, b_ra, w_ri, b_ri, lam, w_out, tc=256):
    b, s, d = h.shape
    d_rnn = conv_w.shape[1]
    full = lambda a: pl.BlockSpec(a.shape, lambda bb, j, nd=a.ndim: (0,) * nd)
    row = lambda v: v.astype(F32).reshape(1, -1)
    args = [norm_g.astype(F32).reshape(1, d), w_in.astype(BF16), conv_w.astype(F32), row(conv_b),
            w_ra.astype(BF16), row(b_ra), w_ri.astype(BF16), row(b_ri), row(lam), w_out.astype(BF16)]
    return pl.pallas_call(
        functools.partial(_odd_mixer_kernel, tc=tc),
        out_shape=jax.ShapeDtypeStruct((b, s, d), F32),
        grid=(b, s // tc),
        in_specs=[pl.BlockSpec((None, tc, d), lambda bb, j: (bb, j, 0))] + [full(a) for a in args],
        out_specs=pl.BlockSpec((None, tc, d), lambda bb, j: (bb, j, 0)),
        scratch_shapes=[pltpu.VMEM((tc + 2 * SUBLANES, d_rnn), F32), pltpu.VMEM((1, d_rnn), F32)],
        compiler_params=_cparams("parallel", "arbitrary"),
        name="odd_mixer",
    )(h, *args)


def _rmsnorm_kernel(x_ref, g_ref, o_ref):
    o_ref[...] = _rmsnorm_rows(x_ref[...], g_ref[...])


def rmsnorm_rows(x2d, g, tm=512):
    n, d = x2d.shape
    return pl.pallas_call(
        _rmsnorm_kernel,
        out_shape=jax.ShapeDtypeStruct((n, d), F32),
        grid=(n // tm,),
        in_specs=[pl.BlockSpec((tm, d), lambda i: (i, 0)), pl.BlockSpec((1, d), lambda i: (0, 0))],
        out_specs=pl.BlockSpec((tm, d), lambda i: (i, 0)),
        compiler_params=_cparams("parallel"),
        name="final_rmsnorm",
    )(x2d, g.astype(F32).reshape(1, d))


def even_mixer_residual(h, norm_g, w_in, b_f, cmp_pos, w_cmp_k, w_cmp_v, w_out):
    b, s, d = h.shape
    h2d = h.reshape(b * s, d)
    w_ext, wt_ext = _even_w_ext(w_in)
    outs = norm_proj(h2d, norm_g.astype(F32), w_ext, _EVEN_SEGS, _EVEN_DTYPES, wt_ext, _EVEN_TSEGS, name="even_in_proj")
    fq, fk, nq, ncmp, nk, small = [o.reshape(b, s, -1) for o in outs[:6]]
    fvt, svt, wvt = outs[6:]
    c = fox_gate_cumsum(small[..., :N_FOX_HEADS].transpose(0, 2, 1), b_f)
    o_fox = fox_attention(fq, fk, fvt, c)
    kc, vct = nsa_compress(ncmp, cmp_pos, w_cmp_k, w_cmp_v)
    gates_t = small[..., N_FOX_HEADS:N_FOX_HEADS + NSA_GATES].reshape(b, s, N_NSA_KV, 3 * NSA_GROUP).transpose(0, 2, 3, 1)
    gates_t = jnp.pad(gates_t, ((0, 0), (0, 0), (0, 2 * SUBLANES - 3 * NSA_GROUP), (0, 0)))
    o_nsa = nsa_attention(nq, kc, vct, nk, svt, wvt, gates_t)
    nf = N_FOX_HEADS * HEAD_DIM
    w_n = w_out[nf:].reshape(N_NSA_HEADS, HEAD_DIM, d)
    half_n = (np.arange(N_NSA_HEADS) // NSA_GROUP)[:, None, None, None] == np.arange(2)[None, :, None, None]
    w_n = jnp.where(half_n, w_n[:, None], 0.0).reshape(N_NSA_HEADS * LANES, d)
    pairs = [(o_fox.reshape(b * s, -1), w_out[:nf].astype(BF16)), (o_nsa.reshape(b * s, -1), w_n.astype(BF16))]
    return proj_residual(pairs, h2d, name="even_out_proj").reshape(b, s, d)


def peer_ffn_residual(h2d, norm_g, w_q, sub_keys, u_tab, v_tab):
    d = h2d.shape[1]
    nq = w_q.shape[1]
    q, xn = norm_proj(h2d, norm_g, w_q.astype(BF16), ((0, nq, 1.0), (None, d, 1.0)), (BF16, BF16), name="peer_query")
    a_idx, b_idx, gw = peer_topk(q, sub_keys)
    return peer_apply(xn, u_tab.astype(BF16).T, v_tab.astype(BF16), a_idx, b_idx, gw, h2d)


def kernel(x, even_norm_g, even_w_in, even_b_f, even_cmp_pos, even_w_cmp_k, even_w_cmp_v, even_w_out, odd_norm_g, odd_w_in, odd_conv_w, odd_conv_b, odd_w_ra, odd_b_ra, odd_w_ri, odd_b_ri, odd_lam, odd_w_out, ffn_norm_g, peer_w_q, peer_sub_keys, peer_u, peer_v, final_g):
    b, s, d = x.shape
    depth = ffn_norm_g.shape[0]
    h = x
    for layer in range(depth):
        j = layer // 2
        if layer % 2 == 0:
            h = even_mixer_residual(h, even_norm_g[j], even_w_in[j], even_b_f[j], even_cmp_pos[j],
                                    even_w_cmp_k[j], even_w_cmp_v[j], even_w_out[j])
        else:
            h = odd_mixer_residual(h, odd_norm_g[j], odd_w_in[j], odd_conv_w[j], odd_conv_b[j], odd_w_ra[j],
                                   odd_b_ra[j], odd_w_ri[j], odd_b_ri[j], odd_lam[j], odd_w_out[j])
        h = peer_ffn_residual(h.reshape(b * s, d), ffn_norm_g[layer].astype(F32), peer_w_q[layer],
                              peer_sub_keys[layer], peer_u[layer], peer_v[layer]).reshape(b, s, d)
    return rmsnorm_rows(h.reshape(b * s, d), final_g).reshape(b, s, d)
```

```python
import functools
import math

import jax
import jax.numpy as jnp
import numpy as np
from jax import lax
from jax.experimental import pallas as pl
from jax.experimental.pallas import tpu as pltpu

HEAD_DIM = 64
N_FOX_HEADS = 8
N_NSA_HEADS = 8
N_NSA_KV = 2
NSA_GROUP = N_NSA_HEADS // N_NSA_KV
CMP_LEN = 32
CMP_STRIDE = 16
SLC_LEN = 64
SLC_TOPK = 16
WINDOW = 512
FORCE_BONUS = 1.0e4
NEG = -1.0e30
D_RNN = 1280
RNN_BLOCKS = 10
RNN_BLOCK_W = D_RNN // RNN_BLOCKS
CONV_W = 4
LRU_C = 8.0
PEER_HEADS = 8
PEER_NKEYS = 128
PEER_HALF = 128
PEER_TOPK = 16
RMS_EPS = 1e-6

LANES = 128
SUBLANES = 8
VMEM_LIMIT = 48 * 1024 * 1024

F32 = jnp.float32
BF16 = jnp.bfloat16


def _cparams(*sem):
    return pltpu.CompilerParams(dimension_semantics=sem, vmem_limit_bytes=VMEM_LIMIT)


def _dot(a, b):
    return lax.dot_general(a, b, (((1,), (0,)), ((), ())), preferred_element_type=F32)


def _dot_nt(a, b):
    return lax.dot_general(a, b, (((1,), (1,)), ((), ())), preferred_element_type=F32)


def _rmsnorm_rows(x, g):
    return x * lax.rsqrt(jnp.mean(x * x, axis=-1, keepdims=True) + RMS_EPS) * g


def _gelu(x):
    return 0.5 * x * (1.0 + lax.erf(x * (1.0 / math.sqrt(2.0))))


def _sigmoid(x):
    return 1.0 / (1.0 + jnp.exp(-x))


def _norm_proj_kernel(x_ref, g_ref, w_ref, wt_ref, *out_refs, segs, tsegs):
    xn = _rmsnorm_rows(x_ref[...], g_ref[...]).astype(BF16)
    for o_ref, (start, width, scale) in zip(out_refs, segs):
        if start is None:
            o_ref[...] = xn
            continue
        y = _dot(xn, w_ref[:, start:start + width])
        if scale != 1.0:
            y = y * scale
        o_ref[...] = y.astype(o_ref.dtype)
    for o_ref, (start, width) in zip(out_refs[len(segs):], tsegs):
        o_ref[...] = _dot_nt(wt_ref[start:start + width, :], xn).astype(o_ref.dtype)


def norm_proj(x2d, g, w_bf16, segs, dtypes, wt_bf16=None, tsegs=(), tm=256, name="norm_proj"):
    n, d = x2d.shape
    if wt_bf16 is None:
        wt_bf16 = jnp.zeros((SUBLANES, d), BF16)
    outs = [jax.ShapeDtypeStruct((n, wd), dt) for (_, wd, _), dt in zip(segs, dtypes)]
    outs += [jax.ShapeDtypeStruct((wd, n), BF16) for (_, wd) in tsegs]
    return pl.pallas_call(
        functools.partial(_norm_proj_kernel, segs=tuple(segs), tsegs=tuple(tsegs)),
        out_shape=outs,
        grid=(n // tm,),
        in_specs=[pl.BlockSpec((tm, d), lambda i: (i, 0)),
                  pl.BlockSpec((1, d), lambda i: (0, 0)),
                  pl.BlockSpec(w_bf16.shape, lambda i: (0, 0)),
                  pl.BlockSpec(wt_bf16.shape, lambda i: (0, 0))],
        out_specs=([pl.BlockSpec((tm, wd), lambda i: (i, 0)) for (_, wd, _) in segs]
                   + [pl.BlockSpec((wd, tm), lambda i: (0, i)) for (_, wd) in tsegs]),
        compiler_params=_cparams("parallel"),
        name=name,
    )(x2d, g.reshape(1, d), w_bf16, wt_bf16)


def _proj_residual_kernel(*refs):
    r_ref, o_ref = refs[-2], refs[-1]
    acc = r_ref[...]
    for a_ref, w_ref in zip(refs[0:-2:2], refs[1:-2:2]):
        acc = acc + _dot(a_ref[...], w_ref[...])
    o_ref[...] = acc


def proj_residual(pairs, resid, tm=512, name="proj_residual"):
    n, d = resid.shape
    args, specs = [], []
    for a, w in pairs:
        args += [a, w]
        specs += [pl.BlockSpec((tm, a.shape[1]), lambda i: (i, 0)), pl.BlockSpec(w.shape, lambda i: (0, 0))]
    return pl.pallas_call(
        _proj_residual_kernel,
        out_shape=jax.ShapeDtypeStruct((n, d), F32),
        grid=(n // tm,),
        in_specs=specs + [pl.BlockSpec((tm, d), lambda i: (i, 0))],
        out_specs=pl.BlockSpec((tm, d), lambda i: (i, 0)),
        compiler_params=_cparams("parallel"),
        name=name,
    )(*args, resid)


def _fox_gate_kernel(f_ref, b_ref, c_ref):
    z = f_ref[...] + b_ref[...]
    x = jnp.minimum(z, 0.0) - jnp.log(1.0 + jnp.exp(-jnp.abs(z)))
    s = x.shape[-1]
    lane = lax.broadcasted_iota(jnp.int32, x.shape, 1)
    shift = 1
    while shift < s:
        x = x + jnp.where(lane >= shift, pltpu.roll(x, shift, 1), 0.0)
        shift *= 2
    c_ref[...] = x


def fox_gate_cumsum(f_t, b_f):
    b, h, s = f_t.shape
    return pl.pallas_call(
        _fox_gate_kernel,
        out_shape=jax.ShapeDtypeStruct((b, h, s), F32),
        grid=(b,),
        in_specs=[pl.BlockSpec((None, h, s), lambda i: (i, 0, 0)),
                  pl.BlockSpec((h, 1), lambda i: (0, 0))],
        out_specs=pl.BlockSpec((None, h, s), lambda i: (i, 0, 0)),
        compiler_params=_cparams("parallel"),
        name="fox_gate_cumsum",
    )(f_t, b_f.reshape(h, 1).astype(F32))


ATTN_KB = 512
FOX_KB = 1024


def _fox_attn_kernel(q_ref, k_ref, vt_ref, ct_ref, cs_ref, o_ref, qs_ref, m_ref, l_ref, acc_ref, *, tq):
    KB = min(FOX_KB, k_ref.shape[0])
    i = pl.program_id(2)
    t0 = i * tq
    for hh in range(2):
        qs_ref[hh * tq:(hh + 1) * tq, :] = q_ref[:, hh * LANES:(hh + 1) * LANES]
    qs = qs_ref[...]
    m_ref[...] = jnp.full(m_ref.shape, -jnp.inf, F32)
    l_ref[...] = jnp.zeros(l_ref.shape, F32)
    acc_ref[...] = jnp.zeros(acc_ref.shape, F32)
    t_pos = t0 + lax.broadcasted_iota(jnp.int32, (KB, tq), 1)
    s_in = lax.broadcasted_iota(jnp.int32, (KB, tq), 0)

    def step(kb, masked):
        start = pl.multiple_of(kb * KB, KB)
        st = _dot_nt(k_ref[pl.ds(start, KB), :], qs)
        vt = vt_ref[:, pl.ds(start, KB)]
        cs = cs_ref[pl.ds(start, KB), :]
        for hh in range(2):
            s = st[:, hh * tq:(hh + 1) * tq] + (ct_ref[hh:hh + 1, :] - cs[:, hh:hh + 1])
            if masked:
                s = jnp.where(start + s_in <= t_pos, s, NEG)
            m_old = m_ref[hh:hh + 1, :]
            m_new = jnp.maximum(m_old, jnp.max(s, axis=0, keepdims=True))
            alpha = jnp.exp(m_old - m_new)
            p = jnp.exp(s - m_new)
            l_ref[hh:hh + 1, :] = alpha * l_ref[hh:hh + 1, :] + jnp.sum(p, axis=0, keepdims=True)
            acc_ref[hh] = alpha * acc_ref[hh] + _dot(vt, p.astype(BF16))
            m_ref[hh:hh + 1, :] = m_new

    n_full = t0 // KB

    def full_body(kb, carry):
        step(kb, False)
        return carry

    lax.fori_loop(0, n_full, full_body, 0)
    step(n_full, True)

    row = lax.broadcasted_iota(jnp.int32, (LANES, tq), 0)
    o0 = acc_ref[0] / l_ref[0:1, :]
    o1 = acc_ref[1] / l_ref[1:2, :]
    ot = jnp.where(row < HEAD_DIM, o0, o1)
    for blk in range(tq // LANES):
        o_ref[blk * LANES:(blk + 1) * LANES, :] = ot[:, blk * LANES:(blk + 1) * LANES].T.astype(o_ref.dtype)


def fox_attention(fq, fk, fvt, c, tq=512):
    b, s, _ = fk.shape
    assert min(FOX_KB, s) % tq == 0 and s % min(FOX_KB, s) == 0
    hp = N_FOX_HEADS // 2
    ct = c.reshape(b, hp, 2, s)
    cs = ct.transpose(0, 1, 3, 2)
    return pl.pallas_call(
        functools.partial(_fox_attn_kernel, tq=tq),
        out_shape=jax.ShapeDtypeStruct((b, s, N_FOX_HEADS * HEAD_DIM), BF16),
        grid=(b, hp, s // tq),
        in_specs=[pl.BlockSpec((None, tq, 2 * LANES), lambda bb, p, i: (bb, i, p)),
                  pl.BlockSpec((None, s, LANES), lambda bb, p, i: (bb, 0, p)),
                  pl.BlockSpec((LANES, s), lambda bb, p, i: (p, bb)),
                  pl.BlockSpec((None, None, 2, tq), lambda bb, p, i: (bb, p, 0, i)),
                  pl.BlockSpec((None, None, s, 2), lambda bb, p, i: (bb, p, 0, 0))],
        out_specs=pl.BlockSpec((None, tq, LANES), lambda bb, p, i: (bb, i, p)),
        scratch_shapes=[pltpu.VMEM((2 * tq, LANES), BF16), pltpu.VMEM((2, tq), F32), pltpu.VMEM((2, tq), F32),
                        pltpu.VMEM((2, LANES, tq), F32)],
        compiler_params=_cparams("parallel", "parallel", "arbitrary"),
        name="fox_attention",
    )(fq, fk, fvt, ct, cs)


FOX_QKV = 3 * N_FOX_HEADS * HEAD_DIM
NSA_Q = N_NSA_HEADS * HEAD_DIM
NSA_KV = 6 * N_NSA_KV * HEAD_DIM
NSA_GATES = 3 * N_NSA_HEADS
_QK_SCALE = HEAD_DIM ** -0.5

_EVEN_SEGS = ((0, 1024, _QK_SCALE), (1024, 512, 1.0), (1536, 1024, _QK_SCALE),
              (2560, 256, 1.0), (2816, 256, 1.0), (3072, 128, 1.0))
_EVEN_DTYPES = (BF16, BF16, BF16, F32, BF16, F32)
_EVEN_TSEGS = ((0, 512), (512, 128), (640, 128))


def _even_w_ext(w_in):
    d = w_in.shape[0]
    o = 0
    w_fq = w_in[:, o:o + 512].reshape(d, N_FOX_HEADS, HEAD_DIM); o += 512
    w_fk = w_in[:, o:o + 512]; o += 512
    w_fv = w_in[:, o:o + 512]; o += 512
    w_ff = w_in[:, o:o + N_FOX_HEADS]; o += N_FOX_HEADS
    w_nq = w_in[:, o:o + NSA_Q].reshape(d, N_NSA_HEADS, HEAD_DIM); o += NSA_Q
    w_nkv = w_in[:, o:o + NSA_KV]; o += NSA_KV
    w_ng = w_in[:, o:o + NSA_GATES]
    half_f = (np.arange(N_FOX_HEADS) % 2)[None, :, None, None] == np.arange(2)[None, None, :, None]
    fq = jnp.where(half_f, w_fq[:, :, None, :], 0.0).reshape(d, N_FOX_HEADS * LANES)
    half_n = (np.arange(N_NSA_HEADS) // NSA_GROUP)[None, :, None, None] == np.arange(2)[None, None, :, None]
    nq = jnp.where(half_n, w_nq[:, :, None, :], 0.0).reshape(d, N_NSA_HEADS * LANES)
    small = jnp.concatenate([w_ff, w_ng, jnp.zeros((d, LANES - N_FOX_HEADS - NSA_GATES), w_in.dtype)], axis=1)
    w = jnp.concatenate([fq, w_fk, nq, w_nkv[:, 0:256], w_nkv[:, 256:384], w_nkv[:, 512:640], small], axis=1)
    wt = jnp.concatenate([w_fv, w_nkv[:, 384:512], w_nkv[:, 640:768]], axis=1).T
    return w.astype(BF16), wt.astype(BF16)


def _nsa_compress_kernel(x_ref, pos_ref, w_ref, kc_ref, vct_ref):
    n = x_ref.shape[2]
    for kind in range(2):
        acc = jnp.zeros((n, LANES), F32)
        for g in range(N_NSA_KV):
            x = x_ref[kind, g]
            top = _dot((x + pos_ref[0:1, :]).astype(BF16), w_ref[kind, g, 0])
            bot = _dot((x + pos_ref[1:2, :]).astype(BF16), w_ref[kind, g, 1])
            acc = acc + top + pltpu.roll(bot, n - 1, 0)
        if kind == 0:
            kc_ref[...] = acc.astype(kc_ref.dtype)
        else:
            for blk in range(n // LANES):
                sl = slice(blk * LANES, (blk + 1) * LANES)
                vct_ref[:, sl] = acc[sl, :].T.astype(vct_ref.dtype)


def nsa_compress(ncmp, cmp_pos, w_cmp_k, w_cmp_v):
    b, s, _ = ncmp.shape
    n = s // CMP_STRIDE
    half = CMP_STRIDE * HEAD_DIM
    x = ncmp.reshape(b, n, CMP_STRIDE, 2, N_NSA_KV, HEAD_DIM).transpose(0, 3, 4, 1, 2, 5).reshape(b, 2, N_NSA_KV, n, half)
    pos = cmp_pos.astype(F32).reshape(2, half)
    w = jnp.stack([w_cmp_k, w_cmp_v]).astype(F32).reshape(2, 2, half, HEAD_DIM)
    place = (np.arange(N_NSA_KV)[:, None, None] == np.arange(2)[None, :, None])
    w_pad = jnp.where(place[None, :, None, None], w[:, None, :, :, None, :], 0.0)
    w_pad = w_pad.reshape(2, N_NSA_KV, 2, half, LANES).astype(BF16)
    return pl.pallas_call(
        _nsa_compress_kernel,
        out_shape=[jax.ShapeDtypeStruct((b, n, LANES), BF16), jax.ShapeDtypeStruct((b, LANES, n), BF16)],
        grid=(b,),
        in_specs=[pl.BlockSpec((None, 2, N_NSA_KV, n, half), lambda i: (i, 0, 0, 0, 0)),
                  pl.BlockSpec((2, half), lambda i: (0, 0)),
                  pl.BlockSpec((2, N_NSA_KV, 2, half, LANES), lambda i: (0, 0, 0, 0, 0))],
        out_specs=[pl.BlockSpec((None, n, LANES), lambda i: (i, 0, 0)),
                   pl.BlockSpec((None, LANES, n), lambda i: (i, 0, 0))],
        compiler_params=_cparams("parallel"),
        name="nsa_compress",
    )(x, pos, w_pad)


def _nsa_attn_kernel(q_ref, kc_ref, vct_ref, nk_ref, svt_ref, wvt_ref, gate_ref, slope_ref, ovt_ref, o_ref,
                     q4_ref, m_ref, l_ref, acc_ref, selt_ref, *, tq):
    g = pl.program_id(1)
    i = pl.program_id(2)
    t0 = i * tq
    n_cmp = kc_ref.shape[0]
    R = NSA_GROUP

    for r in range(R):
        q4_ref[r * tq:(r + 1) * tq, :] = q_ref[:, r * LANES:(r + 1) * LANES]
    q4 = q4_ref[...]

    t_c = t0 + lax.broadcasted_iota(jnp.int32, (n_cmp, tq), 1)
    cend = CMP_STRIDE * lax.broadcasted_iota(jnp.int32, (n_cmp, tq), 0) + (CMP_LEN - 1)
    cmask = cend <= t_c
    cdist = (t_c - cend).astype(F32)
    any_c = (t_c[0:1, :] >= CMP_LEN - 1).astype(F32)
    s_c = _dot_nt(kc_ref[...], q4)
    imp_t = jnp.zeros((ovt_ref.shape[0], tq), F32)
    o_cmp = []
    for r in range(R):
        s = s_c[:, r * tq:(r + 1) * tq] - slope_ref[r:r + 1, 0:1] * cdist
        s = jnp.where(cmask, s, NEG)
        e = jnp.exp(s - jnp.max(s, axis=0, keepdims=True))
        p = (e / jnp.sum(e, axis=0, keepdims=True)) * any_c
        pb = p.astype(BF16)
        o_cmp.append(_dot(vct_ref[...], pb))
        imp_t = imp_t + _dot(ovt_ref[...], pb)

    n_sel = ovt_ref.shape[0]
    jrow = lax.broadcasted_iota(jnp.int32, (n_sel, tq), 0)
    cur = (t0 + lax.broadcasted_iota(jnp.int32, (n_sel, tq), 1)) // SLC_LEN
    forced = ((jrow == 0) | (jrow == cur) | (jrow == cur - 1)).astype(F32)
    imp_t = jnp.where(jrow <= cur, imp_t + FORCE_BONUS * forced, -jnp.inf)
    ngrp = n_sel // SUBLANES
    grp = [imp_t[gi * SUBLANES:(gi + 1) * SUBLANES, :] for gi in range(ngrp)]
    sub8 = lax.broadcasted_iota(jnp.int32, (SUBLANES, tq), 0)
    rank = [jnp.zeros((SUBLANES, tq), jnp.int32) for _ in range(ngrp)]
    for jp in range(n_sel):
        other = imp_t[jp:jp + 1, :]
        for gi in range(ngrp):
            lo = gi * SUBLANES
            if lo + SUBLANES - 1 < jp:
                ahead = other > grp[gi]
            elif lo > jp:
                ahead = other >= grp[gi]
            else:
                ahead = (other > grp[gi]) | ((other == grp[gi]) & (sub8 > jp - lo))
            rank[gi] = rank[gi] + ahead.astype(jnp.int32)
    for gi in range(ngrp):
        selt_ref[gi * SUBLANES:(gi + 1) * SUBLANES, :] = (rank[gi] < SLC_TOPK).astype(F32)

    KB = ATTN_KB
    t_pos = t0 + lax.broadcasted_iota(jnp.int32, (KB, tq), 1)
    s_in = lax.broadcasted_iota(jnp.int32, (KB, tq), 0)
    per_kb = KB // SLC_LEN
    m_ref[...] = jnp.full(m_ref.shape, -jnp.inf, F32)
    l_ref[...] = jnp.zeros(l_ref.shape, F32)
    acc_ref[...] = jnp.zeros(acc_ref.shape, F32)

    def sel_body(kb, carry):
        start = pl.multiple_of(kb * KB, KB)
        dist_i = t_pos - (start + s_in)
        dist = dist_i.astype(F32)
        chosen = jnp.concatenate(
            [jnp.broadcast_to(selt_ref[pl.ds(kb * per_kb + c, 1), :], (SLC_LEN, tq)) for c in range(per_kb)], axis=0)
        mask = (dist_i >= 0) & (chosen > 0.5)
        st = _dot_nt(nk_ref[pl.ds(start, KB), 0:LANES], q4)
        vt = svt_ref[:, pl.ds(start, KB)]
        for r in range(R):
            s = st[:, r * tq:(r + 1) * tq] - slope_ref[r:r + 1, 0:1] * dist
            s = jnp.where(mask, s, NEG)
            m_old = m_ref[r:r + 1, :]
            m_new = jnp.maximum(m_old, jnp.max(s, axis=0, keepdims=True))
            alpha = jnp.exp(m_old - m_new)
            p = jnp.exp(s - m_new)
            l_ref[r:r + 1, :] = alpha * l_ref[r:r + 1, :] + jnp.sum(p, axis=0, keepdims=True)
            acc_ref[r] = alpha * acc_ref[r] + _dot(vt, p.astype(BF16))
            m_ref[r:r + 1, :] = m_new
        return carry

    lax.fori_loop(0, t0 // KB + 1, sel_body, 0)

    nwb = (WINDOW + tq) // tq
    w_first = i - (nwb - 1)
    starts = [pl.multiple_of(jnp.maximum(w_first + d, 0) * tq, tq) for d in range(nwb)]
    kw = jnp.concatenate([nk_ref[pl.ds(st_, tq), LANES:2 * LANES] for st_ in starts], axis=0)
    vtw = jnp.concatenate([wvt_ref[:, pl.ds(st_, tq)] for st_ in starts], axis=1)
    wl = nwb * tq
    key_pos = w_first * tq + lax.broadcasted_iota(jnp.int32, (wl, tq), 0)
    wdist_i = (t0 + lax.broadcasted_iota(jnp.int32, (wl, tq), 1)) - key_pos
    wmask = (wdist_i >= 0) & (wdist_i < WINDOW) & (key_pos >= 0)
    wdist = wdist_i.astype(F32)
    stw = _dot_nt(kw, q4)
    o_win = []
    for r in range(R):
        s = jnp.where(wmask, stw[:, r * tq:(r + 1) * tq] - slope_ref[r:r + 1, 0:1] * wdist, NEG)
        e = jnp.exp(s - jnp.max(s, axis=0, keepdims=True))
        p = e / jnp.sum(e, axis=0, keepdims=True)
        o_win.append(_dot(vtw, p.astype(BF16)))

    gates = _sigmoid(gate_ref[...])
    row = lax.broadcasted_iota(jnp.int32, (LANES, tq), 0)
    mine = (row // HEAD_DIM) == g
    for r in range(R):
        ot = (gates[3 * r:3 * r + 1, :] * o_cmp[r] + gates[3 * r + 1:3 * r + 2, :] * (acc_ref[r] / l_ref[r:r + 1, :])
              + gates[3 * r + 2:3 * r + 3, :] * o_win[r])
        ot = jnp.where(mine, ot, 0.0)
        for blk in range(tq // LANES):
            o_ref[blk * LANES:(blk + 1) * LANES, r * LANES:(r + 1) * LANES] = (
                ot[:, blk * LANES:(blk + 1) * LANES].T.astype(o_ref.dtype))


def _alibi_slopes(n):
    return np.asarray(2.0 ** (-8.0 * np.arange(1, n + 1) / n), np.float32)


def nsa_attention(nq, kc, vct, nk, svt, wvt, gates_t, tq=256):
    b, s, _ = nk.shape
    assert ATTN_KB % tq == 0 and s % ATTN_KB == 0 and WINDOW % tq == 0
    n_cmp = s // CMP_STRIDE
    n_sel = s // SLC_LEN
    slopes = np.broadcast_to(_alibi_slopes(N_NSA_HEADS).reshape(N_NSA_KV, NSA_GROUP, 1), (N_NSA_KV, NSA_GROUP, LANES))
    slopes = jnp.asarray(np.concatenate([slopes, np.zeros((N_NSA_KV, SUBLANES - NSA_GROUP, LANES), np.float32)], axis=1))
    cstart = CMP_STRIDE * np.arange(n_cmp)
    sstart = SLC_LEN * np.arange(n_sel)
    ov = ((cstart[None, :] < sstart[:, None] + SLC_LEN) & (cstart[None, :] + CMP_LEN > sstart[:, None])
          & (cstart[None, :] + CMP_LEN <= s))
    ovt = jnp.asarray(ov, BF16)
    return pl.pallas_call(
        functools.partial(_nsa_attn_kernel, tq=tq),
        out_shape=jax.ShapeDtypeStruct((b, s, N_NSA_HEADS * LANES), BF16),
        grid=(b, N_NSA_KV, s // tq),
        in_specs=[pl.BlockSpec((None, tq, NSA_GROUP * LANES), lambda bb, g, i: (bb, i, g)),
                  pl.BlockSpec((None, n_cmp, LANES), lambda bb, g, i: (bb, 0, 0)),
                  pl.BlockSpec((None, LANES, n_cmp), lambda bb, g, i: (bb, 0, 0)),
                  pl.BlockSpec((None, s, 2 * LANES), lambda bb, g, i: (bb, 0, 0)),
                  pl.BlockSpec((LANES, s), lambda bb, g, i: (0, bb)),
                  pl.BlockSpec((LANES, s), lambda bb, g, i: (0, bb)),
                  pl.BlockSpec((None, None, 2 * SUBLANES, tq), lambda bb, g, i: (bb, g, 0, i)),
                  pl.BlockSpec((None, SUBLANES, LANES), lambda bb, g, i: (g, 0, 0)),
                  pl.BlockSpec((n_sel, n_cmp), lambda bb, g, i: (0, 0))],
        out_specs=pl.BlockSpec((None, tq, NSA_GROUP * LANES), lambda bb, g, i: (bb, i, g)),
        scratch_shapes=[pltpu.VMEM((NSA_GROUP * tq, LANES), BF16),
                        pltpu.VMEM((NSA_GROUP, tq), F32), pltpu.VMEM((NSA_GROUP, tq), F32),
                        pltpu.VMEM((NSA_GROUP, LANES, tq), F32), pltpu.VMEM((n_sel, tq), F32)],
        compiler_params=_cparams("parallel", "parallel", "arbitrary"),
        name="nsa_attention",
    )(nq, kc, vct, nk, svt, wvt, gates_t, slopes, ovt)


def _peer_cand_layout():
    K = PEER_TOPK
    groups = [((0, 1), (0, K))]
    groups += [((j1, j1 + 1), (0, SUBLANES)) for j1 in range(1, SUBLANES)]
    groups += [((SUBLANES, K), (0, 1))]
    flat = []
    for (a0, a1), (b0, b1) in groups:
        flat += [j1 * K + j2 for j1 in range(a0, a1) for j2 in range(b0, b1)]
    return groups, np.asarray(flat, np.int32)


def _extract_top(vals, tags, n, write):
    big = jnp.int32(2 ** 30)
    for it in range(n):
        m = jnp.max(vals, axis=0, keepdims=True)
        tag = jnp.min(jnp.where(vals == m, tags, big), axis=0, keepdims=True)
        write(it, m, tag)
        vals = jnp.where(tags == tag, -jnp.inf, vals)


def _oddeven_merge_sort_pairs(n):
    pairs = []

    def merge(lo, hi, r):
        step = r * 2
        if step < hi - lo:
            merge(lo, hi, step)
            merge(lo + r, hi, step)
            pairs.extend((i, i + r) for i in range(lo + r, hi - r, step))
        else:
            pairs.append((lo, lo + r))

    def sort(lo, hi):
        if hi - lo >= 1:
            mid = lo + (hi - lo) // 2
            sort(lo, mid)
            sort(mid + 1, hi)
            merge(lo, hi, 1)

    sort(0, n - 1)
    return pairs


def _compare_exchange(v, ix, a, b):
    first = v[a] >= v[b]
    v[a], v[b] = jnp.maximum(v[a], v[b]), jnp.minimum(v[a], v[b])
    ix[a], ix[b] = jnp.where(first, ix[a], ix[b]), jnp.where(first, ix[b], ix[a])


def _sorted_top16_network(scores):
    n = PEER_TOPK
    t = scores.shape[1]
    orig = [scores[r * SUBLANES:(r + 1) * SUBLANES, :] for r in range(n)]
    v = list(orig)
    sub = lax.broadcasted_iota(jnp.int32, (SUBLANES, t), 0)
    ix = [sub + r * SUBLANES for r in range(n)]
    for a, b in _oddeven_merge_sort_pairs(n):
        _compare_exchange(v, ix, a, b)
    shift = SUBLANES // 2
    while shift >= 1:
        bv = [pltpu.roll(x, shift, 0) for x in v]
        bi = [pltpu.roll(x, shift, 0) for x in ix]
        for r in range(n):
            o = n - 1 - r
            first = v[r] >= bv[o]
            ix[r] = jnp.where(first, ix[r], bi[o])
            v[r] = jnp.maximum(v[r], bv[o])
        d = n // 2
        while d >= 1:
            for r in range(n):
                if r & d == 0:
                    _compare_exchange(v, ix, r, r + d)
            d //= 2
        shift //= 2
    strict = v[0] > v[1]
    for r in range(1, n - 1):
        strict = strict & (v[r] > v[r + 1])
    at_least = jnp.zeros((SUBLANES, t), jnp.int32)
    for r in range(n):
        at_least = at_least + (orig[r] >= v[n - 1]).astype(jnp.int32)
    sure = strict[0:1, :] & (jnp.sum(at_least, axis=0, keepdims=True) == n)
    return [x[0:1, :] for x in v], [x[0:1, :] for x in ix], sure


def _peer_topk_kernel(q_ref, keys_ref, flat_ref, a_ref, b_ref, g_ref,
                      s1_ref, s2_ref, i1_ref, i2_ref, bs_ref, bid_ref, at_ref, bt_ref, gt_ref, *, tm, groups):
    K = PEER_TOPK
    kidx = lax.broadcasted_iota(jnp.int32, (PEER_NKEYS, tm), 0)
    flat = flat_ref[...]
    for h in range(PEER_HEADS):
        halves = []
        for c, (s_ref, i_ref) in enumerate(((s1_ref, i1_ref), (s2_ref, i2_ref))):
            blk = (2 * h + c) * LANES
            scores = _dot_nt(keys_ref[h, c], q_ref[:, blk:blk + LANES])

            def write(it, m, tag, s_ref=s_ref, i_ref=i_ref):
                s_ref[it:it + 1, :] = m
                i_ref[it:it + 1, :] = tag

            vals, idxs, sure = _sorted_top16_network(scores)
            for it in range(K):
                write(it, vals[it], idxs[it])
            halves.append((scores, write, sure))
        unsure = jnp.max(jnp.where(halves[0][2] & halves[1][2], 0, 1))

        @pl.when(unsure > 0)
        def _(halves=halves):
            for scores, write, _ in halves:
                _extract_top(scores, kidx, K, write)
        s1, s2, i1, i2 = s1_ref[...], s2_ref[...], i1_ref[...], i2_ref[...]
        cand, ids = [], []
        for (a0, a1), (b0, b1) in groups:
            cand.append(s1[a0:a1] + s2[b0:b1])
            ids.append(i1[a0:a1] * PEER_NKEYS + i2[b0:b1])
        cand = jnp.concatenate(cand, axis=0)
        tags = flat * (PEER_NKEYS * PEER_NKEYS) + jnp.concatenate(ids, axis=0)

        def write2(it, m, tag):
            bs_ref[it:it + 1, :] = m
            bid_ref[it:it + 1, :] = tag

        _extract_top(cand, tags, K, write2)
        bs = bs_ref[...]
        e = jnp.exp(bs - jnp.max(bs, axis=0, keepdims=True))
        gt_ref[h * K:(h + 1) * K, :] = e / jnp.sum(e, axis=0, keepdims=True)
        bid = bid_ref[...]
        at_ref[h * K:(h + 1) * K, :] = (bid // PEER_NKEYS) % PEER_NKEYS
        bt_ref[h * K:(h + 1) * K, :] = bid % PEER_NKEYS
    for blk in range(tm // LANES):
        sl = slice(blk * LANES, (blk + 1) * LANES)
        a_ref[sl, :] = at_ref[:, sl].T
        b_ref[sl, :] = bt_ref[:, sl].T
        g_ref[sl, :] = gt_ref[:, sl].T


def peer_topk(q_bf16, sub_keys, tm=512):
    n = q_bf16.shape[0]
    groups, flat = _peer_cand_layout()
    nc = flat.shape[0]
    flat_b = jnp.asarray(np.broadcast_to(flat[:, None], (nc, tm)))
    hk = PEER_HEADS * PEER_TOPK
    row = lambda dt: jax.ShapeDtypeStruct((n, hk), dt)
    return pl.pallas_call(
        functools.partial(_peer_topk_kernel, tm=tm, groups=groups),
        out_shape=[row(jnp.int32), row(jnp.int32), row(F32)],
        grid=(n // tm,),
        in_specs=[pl.BlockSpec((tm, q_bf16.shape[1]), lambda i: (i, 0)),
                  pl.BlockSpec(sub_keys.shape, lambda i: (0, 0, 0, 0)),
                  pl.BlockSpec((nc, tm), lambda i: (0, 0))],
        out_specs=[pl.BlockSpec((tm, hk), lambda i: (i, 0))] * 3,
        scratch_shapes=[pltpu.VMEM((PEER_TOPK, tm), F32), pltpu.VMEM((PEER_TOPK, tm), F32),
                        pltpu.VMEM((PEER_TOPK, tm), jnp.int32), pltpu.VMEM((PEER_TOPK, tm), jnp.int32),
                        pltpu.VMEM((PEER_TOPK, tm), F32), pltpu.VMEM((PEER_TOPK, tm), jnp.int32),
                        pltpu.VMEM((hk, tm), jnp.int32), pltpu.VMEM((hk, tm), jnp.int32),
                        pltpu.VMEM((hk, tm), F32)],
        compiler_params=_cparams("parallel"),
        name="peer_topk",
    )(q_bf16, sub_keys.astype(BF16), flat_b)


_W_ROWS = PEER_NKEYS
_W_PITCH = _W_ROWS + SUBLANES


def _peer_apply_kernel(x_ref, ut_ref, v_ref, a_ref, b_ref, g_ref, r_ref, o_ref, w_ref, *, tm, ec, unroll):
    c = pl.program_id(1)
    slabs = ec // PEER_NKEYS
    steps = _W_ROWS // slabs
    half = c // steps
    cc = c % steps

    @pl.when(c == 0)
    def _():
        o_ref[...] = r_ref[...]

    @pl.when(cc == 0)
    def _():
        sub_a = half * _W_ROWS + lax.broadcasted_iota(jnp.int32, (_W_ROWS, LANES), 0)
        sub_b = lax.broadcasted_iota(jnp.int32, (PEER_NKEYS, LANES), 0)

        def build(tb, carry):
            for tt in range(unroll):
                t = tb * unroll + tt
                a = a_ref[pl.ds(t, 1), :]
                b = b_ref[pl.ds(t, 1), :]
                g = g_ref[pl.ds(t, 1), :]
                pt = jnp.where(sub_a == a, g, 0.0).astype(BF16)
                qt = jnp.where(sub_b == b, 1.0, 0.0).astype(BF16)
                w_ref[pl.ds(pl.multiple_of(t * _W_PITCH, SUBLANES), _W_ROWS), :] = _dot_nt(pt, qt)
            return carry

        lax.fori_loop(0, tm // unroll, build, 0)

    act = _dot(x_ref[...], ut_ref[...])
    w = jnp.concatenate([w_ref[pl.ds(cc * slabs + s, tm, stride=_W_PITCH), :] for s in range(slabs)], axis=1)
    o_ref[...] += _dot((_gelu(act) * w).astype(BF16), v_ref[...])


def peer_apply(xn_bf16, ut_bf16, v_bf16, a_idx, b_idx, gw, resid, tm=256, ec=2048, unroll=32):
    n, d = xn_bf16.shape
    ne = v_bf16.shape[0]
    assert ne == PEER_NKEYS * PEER_NKEYS and _W_ROWS % (ec // PEER_NKEYS) == 0
    tok = lambda w: pl.BlockSpec((tm, w), lambda i, c: (i, 0))
    return pl.pallas_call(
        functools.partial(_peer_apply_kernel, tm=tm, ec=ec, unroll=unroll),
        out_shape=jax.ShapeDtypeStruct((n, d), F32),
        grid=(n // tm, ne // ec),
        in_specs=[tok(d),
                  pl.BlockSpec((d, ec), lambda i, c: (0, c)),
                  pl.BlockSpec((ec, d), lambda i, c: (c, 0)),
                  tok(a_idx.shape[1]), tok(b_idx.shape[1]), tok(gw.shape[1]), tok(d)],
        out_specs=tok(d),
        scratch_shapes=[pltpu.VMEM((tm * _W_PITCH, LANES), F32)],
        compiler_params=pltpu.CompilerParams(dimension_semantics=("parallel", "arbitrary"),
                                             vmem_limit_bytes=56 * 1024 * 1024),
        name="peer_apply",
    )(xn_bf16, ut_bf16, v_bf16, a_idx, b_idx, gw, resid)


def _odd_mixer_kernel(h_ref, g_ref, win_ref, cw_ref, cb_ref, wra_ref, bra_ref, wri_ref, bri_ref, lam_ref, wout_ref,
                      o_ref, xbuf_ref, carry_ref, *, tc):
    j = pl.program_id(1)
    d_rnn = cw_ref.shape[1]
    pad = SUBLANES

    @pl.when(j == 0)
    def _():
        xbuf_ref[0:pad, :] = jnp.zeros((pad, d_rnn), F32)
        carry_ref[...] = jnp.zeros(carry_ref.shape, F32)

    h_in = h_ref[...]
    z = _dot(_rmsnorm_rows(h_in, g_ref[...]).astype(BF16), win_ref[...])
    gate = z[:, :d_rnn]
    xbuf_ref[pad:pad + tc, :] = z[:, d_rnn:]
    xc = cb_ref[...] + cw_ref[0:1, :] * xbuf_ref[pl.ds(pad - 3, tc), :]
    for w in range(1, CONV_W):
        xc = xc + cw_ref[w:w + 1, :] * xbuf_ref[pl.ds(pad - 3 + w, tc), :]
    xbuf_ref[0:pad, :] = xbuf_ref[tc:tc + pad, :]

    xcb = xc.astype(BF16)
    nb = d_rnn // RNN_BLOCK_W
    blk = lambda w_ref: jnp.concatenate(
        [_dot(xcb[:, n * RNN_BLOCK_W:(n + 1) * RNN_BLOCK_W], w_ref[n]) for n in range(nb)], axis=1)
    r = _sigmoid(blk(wra_ref) + bra_ref[...])
    gi = _sigmoid(blk(wri_ref) + bri_ref[...])
    nl = -lam_ref[...]
    softplus = jnp.maximum(nl, 0.0) + jnp.log(1.0 + jnp.exp(-jnp.abs(nl)))
    log_a = (-LRU_C) * r * softplus
    a = jnp.exp(log_a)
    u = jnp.sqrt(1.0 - jnp.exp(2.0 * log_a)) * (gi * xc)

    row = lax.broadcasted_iota(jnp.int32, (tc, d_rnn), 0)
    shift = 1
    while shift < tc:
        valid = row >= shift
        a_prev = jnp.where(valid, pltpu.roll(a, shift, 0), 1.0)
        u_prev = jnp.where(valid, pltpu.roll(u, shift, 0), 0.0)
        u = u + a * u_prev
        a = a * a_prev
        shift *= 2
    hs = u + a * carry_ref[...]
    carry_ref[...] = hs[tc - 1:tc, :]

    y = (_gelu(gate) * hs).astype(BF16)
    o_ref[...] = h_in + _dot(y, wout_ref[...])


def odd_mixer_residual(h, norm_g, w_in, conv_w, conv_b, w_ra, b_ra, w_ri, b_ri, lam, w_out, tc=256):
    b, s, d = h.shape
    d_rnn = conv_w.shape[1]
    full = lambda a: pl.BlockSpec(a.shape, lambda bb, j, nd=a.ndim: (0,) * nd)
    row = lambda v: v.astype(F32).reshape(1, -1)
    args = [norm_g.astype(F32).reshape(1, d), w_in.astype(BF16), conv_w.astype(F32), row(conv_b),
            w_ra.astype(BF16), row(b_ra), w_ri.astype(BF16), row(b_ri), row(lam), w_out.astype(BF16)]
    return pl.pallas_call(
        functools.partial(_odd_mixer_kernel, tc=tc),
        out_shape=jax.ShapeDtypeStruct((b, s, d), F32),
        grid=(b, s // tc),
        in_specs=[pl.BlockSpec((None, tc, d), lambda bb, j: (bb, j, 0))] + [full(a) for a in args],
        out_specs=pl.BlockSpec((None, tc, d), lambda bb, j: (bb, j, 0)),
        scratch_shapes=[pltpu.VMEM((tc + 2 * SUBLANES, d_rnn), F32), pltpu.VMEM((1, d_rnn), F32)],
        compiler_params=_cparams("parallel", "arbitrary"),
        name="odd_mixer",
    )(h, *args)


def _rmsnorm_kernel(x_ref, g_ref, o_ref):
    o_ref[...] = _rmsnorm_rows(x_ref[...], g_ref[...])


def rmsnorm_rows(x2d, g, tm=512):
    n, d = x2d.shape
    return pl.pallas_call(
        _rmsnorm_kernel,
        out_shape=jax.ShapeDtypeStruct((n, d), F32),
        grid=(n // tm,),
        in_specs=[pl.BlockSpec((tm, d), lambda i: (i, 0)), pl.BlockSpec((1, d), lambda i: (0, 0))],
        out_specs=pl.BlockSpec((tm, d), lambda i: (i, 0)),
        compiler_params=_cparams("parallel"),
        name="final_rmsnorm",
    )(x2d, g.astype(F32).reshape(1, d))


def even_mixer_residual(h, norm_g, w_in, b_f, cmp_pos, w_cmp_k, w_cmp_v, w_out):
    b, s, d = h.shape
    h2d = h.reshape(b * s, d)
    w_ext, wt_ext = _even_w_ext(w_in)
    outs = norm_proj(h2d, norm_g.astype(F32), w_ext, _EVEN_SEGS, _EVEN_DTYPES, wt_ext, _EVEN_TSEGS, name="even_in_proj")
    fq, fk, nq, ncmp, nk, small = [o.reshape(b, s, -1) for o in outs[:6]]
    fvt, svt, wvt = outs[6:]
    c = fox_gate_cumsum(small[..., :N_FOX_HEADS].transpose(0, 2, 1), b_f)
    o_fox = fox_attention(fq, fk, fvt, c)
    kc, vct = nsa_compress(ncmp, cmp_pos, w_cmp_k, w_cmp_v)
    gates_t = small[..., N_FOX_HEADS:N_FOX_HEADS + NSA_GATES].reshape(b, s, N_NSA_KV, 3 * NSA_GROUP).transpose(0, 2, 3, 1)
    gates_t = jnp.pad(gates_t, ((0, 0), (0, 0), (0, 2 * SUBLANES - 3 * NSA_GROUP), (0, 0)))
    o_nsa = nsa_attention(nq, kc, vct, nk, svt, wvt, gates_t)
    nf = N_FOX_HEADS * HEAD_DIM
    w_n = w_out[nf:].reshape(N_NSA_HEADS, HEAD_DIM, d)
    half_n = (np.arange(N_NSA_HEADS) // NSA_GROUP)[:, None, None, None] == np.arange(2)[None, :, None, None]
    w_n = jnp.where(half_n, w_n[:, None], 0.0).reshape(N_NSA_HEADS * LANES, d)
    pairs = [(o_fox.reshape(b * s, -1), w_out[:nf].astype(BF16)), (o_nsa.reshape(b * s, -1), w_n.astype(BF16))]
    return proj_residual(pairs, h2d, name="even_out_proj").reshape(b, s, d)


def peer_ffn_residual(h2d, norm_g, w_q, sub_keys, u_tab, v_tab):
    d = h2d.shape[1]
    nq = w_q.shape[1]
    q, xn = norm_proj(h2d, norm_g, w_q.astype(BF16), ((0, nq, 1.0), (None, d, 1.0)), (BF16, BF16), name="peer_query")
    a_idx, b_idx, gw = peer_topk(q, sub_keys)
    return peer_apply(xn, u_tab.astype(BF16).T, v_tab.astype(BF16), a_idx, b_idx, gw, h2d)


def kernel(x, even_norm_g, even_w_in, even_b_f, even_cmp_pos, even_w_cmp_k, even_w_cmp_v, even_w_out, odd_norm_g, odd_w_in, odd_conv_w, odd_conv_b, odd_w_ra, odd_b_ra, odd_w_ri, odd_b_ri, odd_lam, odd_w_out, ffn_norm_g, peer_w_q, peer_sub_keys, peer_u, peer_v, final_g):
    b, s, d = x.shape
    depth = ffn_norm_g.shape[0]
    h = x
    for layer in range(depth):
        j = layer // 2
        if layer % 2 == 0:
            h = even_mixer_residual(h, even_norm_g[j], even_w_in[j], even_b_f[j], even_cmp_pos[j],
                                    even_w_cmp_k[j], even_w_cmp_v[j], even_w_out[j])
        else:
            h = odd_mixer_residual(h, odd_norm_g[j], odd_w_in[j], odd_conv_w[j], odd_conv_b[j], odd_w_ra[j],
                                   odd_b_ra[j], odd_w_ri[j], odd_b_ri[j], odd_lam[j], odd_w_out[j])
        h = peer_ffn_residual(h.reshape(b * s, d), ffn_norm_g[layer].astype(F32), peer_w_q[layer],
                              peer_sub_keys[layer], peer_u[layer], peer_v[layer]).reshape(b, s, d)
    return rmsnorm_rows(h.reshape(b * s, d), final_g).reshape(b, s, d)
```

```python
import functools
import math

import jax
import jax.numpy as jnp
import numpy as np
from jax import lax
from jax.experimental import pallas as pl
from jax.experimental.pallas import tpu as pltpu

HEAD_DIM = 64
N_FOX_HEADS = 8
N_NSA_HEADS = 8
N_NSA_KV = 2
NSA_GROUP = N_NSA_HEADS // N_NSA_KV
CMP_LEN = 32
CMP_STRIDE = 16
SLC_LEN = 64
SLC_TOPK = 16
WINDOW = 512
FORCE_BONUS = 1.0e4
NEG = -1.0e30
D_RNN = 1280
RNN_BLOCKS = 10
RNN_BLOCK_W = D_RNN // RNN_BLOCKS
CONV_W = 4
LRU_C = 8.0
PEER_HEADS = 8
PEER_NKEYS = 128
PEER_HALF = 128
PEER_TOPK = 16
RMS_EPS = 1e-6

LANES = 128
SUBLANES = 8
VMEM_LIMIT = 48 * 1024 * 1024

F32 = jnp.float32
BF16 = jnp.bfloat16


def _cparams(*sem):
    return pltpu.CompilerParams(dimension_semantics=sem, vmem_limit_bytes=VMEM_LIMIT)


def _dot(a, b):
    return lax.dot_general(a, b, (((1,), (0,)), ((), ())), preferred_element_type=F32)


def _dot_nt(a, b):
    return lax.dot_general(a, b, (((1,), (1,)), ((), ())), preferred_element_type=F32)


def _rmsnorm_rows(x, g):
    return x * lax.rsqrt(jnp.mean(x * x, axis=-1, keepdims=True) + RMS_EPS) * g


def _gelu(x):
    return 0.5 * x * (1.0 + lax.erf(x * (1.0 / math.sqrt(2.0))))


def _sigmoid(x):
    return 1.0 / (1.0 + jnp.exp(-x))


def _norm_proj_kernel(x_ref, g_ref, w_ref, wt_ref, *out_refs, segs, tsegs):
    xn = _rmsnorm_rows(x_ref[...], g_ref[...]).astype(BF16)
    for o_ref, (start, width, scale) in zip(out_refs, segs):
        if start is None:
            o_ref[...] = xn
            continue
        y = _dot(xn, w_ref[:, start:start + width])
        if scale != 1.0:
            y = y * scale
        o_ref[...] = y.astype(o_ref.dtype)
    for o_ref, (start, width) in zip(out_refs[len(segs):], tsegs):
        o_ref[...] = _dot_nt(wt_ref[start:start + width, :], xn).astype(o_ref.dtype)


def norm_proj(x2d, g, w_bf16, segs, dtypes, wt_bf16=None, tsegs=(), tm=256, name="norm_proj"):
    n, d = x2d.shape
    if wt_bf16 is None:
        wt_bf16 = jnp.zeros((SUBLANES, d), BF16)
    outs = [jax.ShapeDtypeStruct((n, wd), dt) for (_, wd, _), dt in zip(segs, dtypes)]
    outs += [jax.ShapeDtypeStruct((wd, n), BF16) for (_, wd) in tsegs]
    return pl.pallas_call(
        functools.partial(_norm_proj_kernel, segs=tuple(segs), tsegs=tuple(tsegs)),
        out_shape=outs,
        grid=(n // tm,),
        in_specs=[pl.BlockSpec((tm, d), lambda i: (i, 0)),
                  pl.BlockSpec((1, d), lambda i: (0, 0)),
                  pl.BlockSpec(w_bf16.shape, lambda i: (0, 0)),
                  pl.BlockSpec(wt_bf16.shape, lambda i: (0, 0))],
        out_specs=([pl.BlockSpec((tm, wd), lambda i: (i, 0)) for (_, wd, _) in segs]
                   + [pl.BlockSpec((wd, tm), lambda i: (0, i)) for (_, wd) in tsegs]),
        compiler_params=_cparams("parallel"),
        name=name,
    )(x2d, g.reshape(1, d), w_bf16, wt_bf16)


def _proj_residual_kernel(*refs):
    r_ref, o_ref = refs[-2], refs[-1]
    acc = r_ref[...]
    for a_ref, w_ref in zip(refs[0:-2:2], refs[1:-2:2]):
        acc = acc + _dot(a_ref[...], w_ref[...])
    o_ref[...] = acc


def proj_residual(pairs, resid, tm=512, name="proj_residual"):
    n, d = resid.shape
    args, specs = [], []
    for a, w in pairs:
        args += [a, w]
        specs += [pl.BlockSpec((tm, a.shape[1]), lambda i: (i, 0)), pl.BlockSpec(w.shape, lambda i: (0, 0))]
    return pl.pallas_call(
        _proj_residual_kernel,
        out_shape=jax.ShapeDtypeStruct((n, d), F32),
        grid=(n // tm,),
        in_specs=specs + [pl.BlockSpec((tm, d), lambda i: (i, 0))],
        out_specs=pl.BlockSpec((tm, d), lambda i: (i, 0)),
        compiler_params=_cparams("parallel"),
        name=name,
    )(*args, resid)


def _fox_gate_kernel(f_ref, b_ref, c_ref):
    z = f_ref[...] + b_ref[...]
    x = jnp.minimum(z, 0.0) - jnp.log(1.0 + jnp.exp(-jnp.abs(z)))
    s = x.shape[-1]
    lane = lax.broadcasted_iota(jnp.int32, x.shape, 1)
    shift = 1
    while shift < s:
        x = x + jnp.where(lane >= shift, pltpu.roll(x, shift, 1), 0.0)
        shift *= 2
    c_ref[...] = x


def fox_gate_cumsum(f_t, b_f):
    b, h, s = f_t.shape
    return pl.pallas_call(
        _fox_gate_kernel,
        out_shape=jax.ShapeDtypeStruct((b, h, s), F32),
        grid=(b,),
        in_specs=[pl.BlockSpec((None, h, s), lambda i: (i, 0, 0)),
                  pl.BlockSpec((h, 1), lambda i: (0, 0))],
        out_specs=pl.BlockSpec((None, h, s), lambda i: (i, 0, 0)),
        compiler_params=_cparams("parallel"),
        name="fox_gate_cumsum",
    )(f_t, b_f.reshape(h, 1).astype(F32))


ATTN_KB = 512
FOX_KB = 1024


def _fox_attn_kernel(q_ref, k_ref, vt_ref, ct_ref, cs_ref, o_ref, qs_ref, m_ref, l_ref, acc_ref, *, tq):
    KB = min(FOX_KB, k_ref.shape[0])
    i = pl.program_id(2)
    t0 = i * tq
    for hh in range(2):
        qs_ref[hh * tq:(hh + 1) * tq, :] = q_ref[:, hh * LANES:(hh + 1) * LANES]
    qs = qs_ref[...]
    m_ref[...] = jnp.full(m_ref.shape, -jnp.inf, F32)
    l_ref[...] = jnp.zeros(l_ref.shape, F32)
    acc_ref[...] = jnp.zeros(acc_ref.shape, F32)
    t_pos = t0 + lax.broadcasted_iota(jnp.int32, (KB, tq), 1)
    s_in = lax.broadcasted_iota(jnp.int32, (KB, tq), 0)

    def step(kb, masked):
        start = pl.multiple_of(kb * KB, KB)
        st = _dot_nt(k_ref[pl.ds(start, KB), :], qs)
        vt = vt_ref[:, pl.ds(start, KB)]
        cs = cs_ref[pl.ds(start, KB), :]
        for hh in range(2):
            s = st[:, hh * tq:(hh + 1) * tq] + (ct_ref[hh:hh + 1, :] - cs[:, hh:hh + 1])
            if masked:
                s = jnp.where(start + s_in <= t_pos, s, NEG)
            m_old = m_ref[hh:hh + 1, :]
            m_new = jnp.maximum(m_old, jnp.max(s, axis=0, keepdims=True))
            alpha = jnp.exp(m_old - m_new)
            p = jnp.exp(s - m_new)
            l_ref[hh:hh + 1, :] = alpha * l_ref[hh:hh + 1, :] + jnp.sum(p, axis=0, keepdims=True)
            acc_ref[hh] = alpha * acc_ref[hh] + _dot(vt, p.astype(BF16))
            m_ref[hh:hh + 1, :] = m_new

    n_full = t0 // KB

    def full_body(kb, carry):
        step(kb, False)
        return carry

    lax.fori_loop(0, n_full, full_body, 0)
    step(n_full, True)

    row = lax.broadcasted_iota(jnp.int32, (LANES, tq), 0)
    o0 = acc_ref[0] / l_ref[0:1, :]
    o1 = acc_ref[1] / l_ref[1:2, :]
    ot = jnp.where(row < HEAD_DIM, o0, o1)
    for blk in range(tq // LANES):
        o_ref[blk * LANES:(blk + 1) * LANES, :] = ot[:, blk * LANES:(blk + 1) * LANES].T.astype(o_ref.dtype)


def fox_attention(fq, fk, fvt, c, tq=512):
    b, s, _ = fk.shape
    assert min(FOX_KB, s) % tq == 0 and s % min(FOX_KB, s) == 0
    hp = N_FOX_HEADS // 2
    ct = c.reshape(b, hp, 2, s)
    cs = ct.transpose(0, 1, 3, 2)
    return pl.pallas_call(
        functools.partial(_fox_attn_kernel, tq=tq),
        out_shape=jax.ShapeDtypeStruct((b, s, N_FOX_HEADS * HEAD_DIM), BF16),
        grid=(b, hp, s // tq),
        in_specs=[pl.BlockSpec((None, tq, 2 * LANES), lambda bb, p, i: (bb, i, p)),
                  pl.BlockSpec((None, s, LANES), lambda bb, p, i: (bb, 0, p)),
                  pl.BlockSpec((LANES, s), lambda bb, p, i: (p, bb)),
                  pl.BlockSpec((None, None, 2, tq), lambda bb, p, i: (bb, p, 0, i)),
                  pl.BlockSpec((None, None, s, 2), lambda bb, p, i: (bb, p, 0, 0))],
        out_specs=pl.BlockSpec((None, tq, LANES), lambda bb, p, i: (bb, i, p)),
        scratch_shapes=[pltpu.VMEM((2 * tq, LANES), BF16), pltpu.VMEM((2, tq), F32), pltpu.VMEM((2, tq), F32),
                        pltpu.VMEM((2, LANES, tq), F32)],
        compiler_params=_cparams("parallel", "parallel", "arbitrary"),
        name="fox_attention",
    )(fq, fk, fvt, ct, cs)


FOX_QKV = 3 * N_FOX_HEADS * HEAD_DIM
NSA_Q = N_NSA_HEADS * HEAD_DIM
NSA_KV = 6 * N_NSA_KV * HEAD_DIM
NSA_GATES = 3 * N_NSA_HEADS
_QK_SCALE = HEAD_DIM ** -0.5

_EVEN_SEGS = ((0, 1024, _QK_SCALE), (1024, 512, 1.0), (1536, 1024, _QK_SCALE),
              (2560, 256, 1.0), (2816, 256, 1.0), (3072, 128, 1.0))
_EVEN_DTYPES = (BF16, BF16, BF16, F32, BF16, F32)
_EVEN_TSEGS = ((0, 512), (512, 128), (640, 128))


def _even_w_ext(w_in):
    d = w_in.shape[0]
    o = 0
    w_fq = w_in[:, o:o + 512].reshape(d, N_FOX_HEADS, HEAD_DIM); o += 512
    w_fk = w_in[:, o:o + 512]; o += 512
    w_fv = w_in[:, o:o + 512]; o += 512
    w_ff = w_in[:, o:o + N_FOX_HEADS]; o += N_FOX_HEADS
    w_nq = w_in[:, o:o + NSA_Q].reshape(d, N_NSA_HEADS, HEAD_DIM); o += NSA_Q
    w_nkv = w_in[:, o:o + NSA_KV]; o += NSA_KV
    w_ng = w_in[:, o:o + NSA_GATES]
    half_f = (np.arange(N_FOX_HEADS) % 2)[None, :, None, None] == np.arange(2)[None, None, :, None]
    fq = jnp.where(half_f, w_fq[:, :, None, :], 0.0).reshape(d, N_FOX_HEADS * LANES)
    half_n = (np.arange(N_NSA_HEADS) // NSA_GROUP)[None, :, None, None] == np.arange(2)[None, None, :, None]
    nq = jnp.where(half_n, w_nq[:, :, None, :], 0.0).reshape(d, N_NSA_HEADS * LANES)
    small = jnp.concatenate([w_ff, w_ng, jnp.zeros((d, LANES - N_FOX_HEADS - NSA_GATES), w_in.dtype)], axis=1)
    w = jnp.concatenate([fq, w_fk, nq, w_nkv[:, 0:256], w_nkv[:, 256:384], w_nkv[:, 512:640], small], axis=1)
    wt = jnp.concatenate([w_fv, w_nkv[:, 384:512], w_nkv[:, 640:768]], axis=1).T
    return w.astype(BF16), wt.astype(BF16)


def _nsa_compress_kernel(x_ref, pos_ref, w_ref, kc_ref, vct_ref):
    n = x_ref.shape[2]
    for kind in range(2):
        acc = jnp.zeros((n, LANES), F32)
        for g in range(N_NSA_KV):
            x = x_ref[kind, g]
            top = _dot((x + pos_ref[0:1, :]).astype(BF16), w_ref[kind, g, 0])
            bot = _dot((x + pos_ref[1:2, :]).astype(BF16), w_ref[kind, g, 1])
            acc = acc + top + pltpu.roll(bot, n - 1, 0)
        if kind == 0:
            kc_ref[...] = acc.astype(kc_ref.dtype)
        else:
            for blk in range(n // LANES):
                sl = slice(blk * LANES, (blk + 1) * LANES)
                vct_ref[:, sl] = acc[sl, :].T.astype(vct_ref.dtype)


def nsa_compress(ncmp, cmp_pos, w_cmp_k, w_cmp_v):
    b, s, _ = ncmp.shape
    n = s // CMP_STRIDE
    half = CMP_STRIDE * HEAD_DIM
    x = ncmp.reshape(b, n, CMP_STRIDE, 2, N_NSA_KV, HEAD_DIM).transpose(0, 3, 4, 1, 2, 5).reshape(b, 2, N_NSA_KV, n, half)
    pos = cmp_pos.astype(F32).reshape(2, half)
    w = jnp.stack([w_cmp_k, w_cmp_v]).astype(F32).reshape(2, 2, half, HEAD_DIM)
    place = (np.arange(N_NSA_KV)[:, None, None] == np.arange(2)[None, :, None])
    w_pad = jnp.where(place[None, :, None, None], w[:, None, :, :, None, :], 0.0)
    w_pad = w_pad.reshape(2, N_NSA_KV, 2, half, LANES).astype(BF16)
    return pl.pallas_call(
        _nsa_compress_kernel,
        out_shape=[jax.ShapeDtypeStruct((b, n, LANES), BF16), jax.ShapeDtypeStruct((b, LANES, n), BF16)],
        grid=(b,),
        in_specs=[pl.BlockSpec((None, 2, N_NSA_KV, n, half), lambda i: (i, 0, 0, 0, 0)),
                  pl.BlockSpec((2, half), lambda i: (0, 0)),
                  pl.BlockSpec((2, N_NSA_KV, 2, half, LANES), lambda i: (0, 0, 0, 0, 0))],
        out_specs=[pl.BlockSpec((None, n, LANES), lambda i: (i, 0, 0)),
                   pl.BlockSpec((None, LANES, n), lambda i: (i, 0, 0))],
        compiler_params=_cparams("parallel"),
        name="nsa_compress",
    )(x, pos, w_pad)


def _nsa_attn_kernel(q_ref, kc_ref, vct_ref, nk_ref, svt_ref, wvt_ref, gate_ref, slope_ref, ovt_ref, o_ref,
                     q4_ref, m_ref, l_ref, acc_ref, selt_ref, *, tq):
    g = pl.program_id(1)
    i = pl.program_id(2)
    t0 = i * tq
    n_cmp = kc_ref.shape[0]
    R = NSA_GROUP

    for r in range(R):
        q4_ref[r * tq:(r + 1) * tq, :] = q_ref[:, r * LANES:(r + 1) * LANES]
    q4 = q4_ref[...]

    t_c = t0 + lax.broadcasted_iota(jnp.int32, (n_cmp, tq), 1)
    cend = CMP_STRIDE * lax.broadcasted_iota(jnp.int32, (n_cmp, tq), 0) + (CMP_LEN - 1)
    cmask = cend <= t_c
    cdist = (t_c - cend).astype(F32)
    any_c = (t_c[0:1, :] >= CMP_LEN - 1).astype(F32)
    s_c = _dot_nt(kc_ref[...], q4)
    imp_t = jnp.zeros((ovt_ref.shape[0], tq), F32)
    o_cmp = []
    for r in range(R):
        s = s_c[:, r * tq:(r + 1) * tq] - slope_ref[r:r + 1, 0:1] * cdist
        s = jnp.where(cmask, s, NEG)
        e = jnp.exp(s - jnp.max(s, axis=0, keepdims=True))
        p = (e / jnp.sum(e, axis=0, keepdims=True)) * any_c
        pb = p.astype(BF16)
        o_cmp.append(_dot(vct_ref[...], pb))
        imp_t = imp_t + _dot(ovt_ref[...], pb)

    n_sel = ovt_ref.shape[0]
    jrow = lax.broadcasted_iota(jnp.int32, (n_sel, tq), 0)
    cur = (t0 + lax.broadcasted_iota(jnp.int32, (n_sel, tq), 1)) // SLC_LEN
    forced = ((jrow == 0) | (jrow == cur) | (jrow == cur - 1)).astype(F32)
    imp_t = jnp.where(jrow <= cur, imp_t + FORCE_BONUS * forced, -jnp.inf)
    ngrp = n_sel // SUBLANES
    grp = [imp_t[gi * SUBLANES:(gi + 1) * SUBLANES, :] for gi in range(ngrp)]
    sub8 = lax.broadcasted_iota(jnp.int32, (SUBLANES, tq), 0)
    rank = [jnp.zeros((SUBLANES, tq), jnp.int32) for _ in range(ngrp)]
    for jp in range(n_sel):
        other = imp_t[jp:jp + 1, :]
        for gi in range(ngrp):
            lo = gi * SUBLANES
            if lo + SUBLANES - 1 < jp:
                ahead = other > grp[gi]
            elif lo > jp:
                ahead = other >= grp[gi]
            else:
                ahead = (other > grp[gi]) | ((other == grp[gi]) & (sub8 > jp - lo))
            rank[gi] = rank[gi] + ahead.astype(jnp.int32)
    for gi in range(ngrp):
        selt_ref[gi * SUBLANES:(gi + 1) * SUBLANES, :] = (rank[gi] < SLC_TOPK).astype(F32)

    KB = ATTN_KB
    t_pos = t0 + lax.broadcasted_iota(jnp.int32, (KB, tq), 1)
    s_in = lax.broadcasted_iota(jnp.int32, (KB, tq), 0)
    per_kb = KB // SLC_LEN
    m_ref[...] = jnp.full(m_ref.shape, -jnp.inf, F32)
    l_ref[...] = jnp.zeros(l_ref.shape, F32)
    acc_ref[...] = jnp.zeros(acc_ref.shape, F32)

    def sel_body(kb, carry):
        start = pl.multiple_of(kb * KB, KB)
        dist_i = t_pos - (start + s_in)
        dist = dist_i.astype(F32)
        chosen = jnp.concatenate(
            [jnp.broadcast_to(selt_ref[pl.ds(kb * per_kb + c, 1), :], (SLC_LEN, tq)) for c in range(per_kb)], axis=0)
        mask = (dist_i >= 0) & (chosen > 0.5)
        st = _dot_nt(nk_ref[pl.ds(start, KB), 0:LANES], q4)
        vt = svt_ref[:, pl.ds(start, KB)]
        for r in range(R):
            s = st[:, r * tq:(r + 1) * tq] - slope_ref[r:r + 1, 0:1] * dist
            s = jnp.where(mask, s, NEG)
            m_old = m_ref[r:r + 1, :]
            m_new = jnp.maximum(m_old, jnp.max(s, axis=0, keepdims=True))
            alpha = jnp.exp(m_old - m_new)
            p = jnp.exp(s - m_new)
            l_ref[r:r + 1, :] = alpha * l_ref[r:r + 1, :] + jnp.sum(p, axis=0, keepdims=True)
            acc_ref[r] = alpha * acc_ref[r] + _dot(vt, p.astype(BF16))
            m_ref[r:r + 1, :] = m_new
        return carry

    lax.fori_loop(0, t0 // KB + 1, sel_body, 0)

    nwb = (WINDOW + tq) // tq
    w_first = i - (nwb - 1)
    starts = [pl.multiple_of(jnp.maximum(w_first + d, 0) * tq, tq) for d in range(nwb)]
    kw = jnp.concatenate([nk_ref[pl.ds(st_, tq), LANES:2 * LANES] for st_ in starts], axis=0)
    vtw = jnp.concatenate([wvt_ref[:, pl.ds(st_, tq)] for st_ in starts], axis=1)
    wl = nwb * tq
    key_pos = w_first * tq + lax.broadcasted_iota(jnp.int32, (wl, tq), 0)
    wdist_i = (t0 + lax.broadcasted_iota(jnp.int32, (wl, tq), 1)) - key_pos
    wmask = (wdist_i >= 0) & (wdist_i < WINDOW) & (key_pos >= 0)
    wdist = wdist_i.astype(F32)
    stw = _dot_nt(kw, q4)
    o_win = []
    for r in range(R):
        s = jnp.where(wmask, stw[:, r * tq:(r + 1) * tq] - slope_ref[r:r + 1, 0:1] * wdist, NEG)
        e = jnp.exp(s - jnp.max(s, axis=0, keepdims=True))
        p = e / jnp.sum(e, axis=0, keepdims=True)
        o_win.append(_dot(vtw, p.astype(BF16)))

    gates = _sigmoid(gate_ref[...])
    row = lax.broadcasted_iota(jnp.int32, (LANES, tq), 0)
    mine = (row // HEAD_DIM) == g
    for r in range(R):
        ot = (gates[3 * r:3 * r + 1, :] * o_cmp[r] + gates[3 * r + 1:3 * r + 2, :] * (acc_ref[r] / l_ref[r:r + 1, :])
              + gates[3 * r + 2:3 * r + 3, :] * o_win[r])
        ot = jnp.where(mine, ot, 0.0)
        for blk in range(tq // LANES):
            o_ref[blk * LANES:(blk + 1) * LANES, r * LANES:(r + 1) * LANES] = (
                ot[:, blk * LANES:(blk + 1) * LANES].T.astype(o_ref.dtype))


def _alibi_slopes(n):
    return np.asarray(2.0 ** (-8.0 * np.arange(1, n + 1) / n), np.float32)


def nsa_attention(nq, kc, vct, nk, svt, wvt, gates_t, tq=256):
    b, s, _ = nk.shape
    assert ATTN_KB % tq == 0 and s % ATTN_KB == 0 and WINDOW % tq == 0
    n_cmp = s // CMP_STRIDE
    n_sel = s // SLC_LEN
    slopes = np.broadcast_to(_alibi_slopes(N_NSA_HEADS).reshape(N_NSA_KV, NSA_GROUP, 1), (N_NSA_KV, NSA_GROUP, LANES))
    slopes = jnp.asarray(np.concatenate([slopes, np.zeros((N_NSA_KV, SUBLANES - NSA_GROUP, LANES), np.float32)], axis=1))
    cstart = CMP_STRIDE * np.arange(n_cmp)
    sstart = SLC_LEN * np.arange(n_sel)
    ov = ((cstart[None, :] < sstart[:, None] + SLC_LEN) & (cstart[None, :] + CMP_LEN > sstart[:, None])
          & (cstart[None, :] + CMP_LEN <= s))
    ovt = jnp.asarray(ov, BF16)
    return pl.pallas_call(
        functools.partial(_nsa_attn_kernel, tq=tq),
        out_shape=jax.ShapeDtypeStruct((b, s, N_NSA_HEADS * LANES), BF16),
        grid=(b, N_NSA_KV, s // tq),
        in_specs=[pl.BlockSpec((None, tq, NSA_GROUP * LANES), lambda bb, g, i: (bb, i, g)),
                  pl.BlockSpec((None, n_cmp, LANES), lambda bb, g, i: (bb, 0, 0)),
                  pl.BlockSpec((None, LANES, n_cmp), lambda bb, g, i: (bb, 0, 0)),
                  pl.BlockSpec((None, s, 2 * LANES), lambda bb, g, i: (bb, 0, 0)),
                  pl.BlockSpec((LANES, s), lambda bb, g, i: (0, bb)),
                  pl.BlockSpec((LANES, s), lambda bb, g, i: (0, bb)),
                  pl.BlockSpec((None, None, 2 * SUBLANES, tq), lambda bb, g, i: (bb, g, 0, i)),
                  pl.BlockSpec((None, SUBLANES, LANES), lambda bb, g, i: (g, 0, 0)),
                  pl.BlockSpec((n_sel, n_cmp), lambda bb, g, i: (0, 0))],
        out_specs=pl.BlockSpec((None, tq, NSA_GROUP * LANES), lambda bb, g, i: (bb, i, g)),
        scratch_shapes=[pltpu.VMEM((NSA_GROUP * tq, LANES), BF16),
                        pltpu.VMEM((NSA_GROUP, tq), F32), pltpu.VMEM((NSA_GROUP, tq), F32),
                        pltpu.VMEM((NSA_GROUP, LANES, tq), F32), pltpu.VMEM((n_sel, tq), F32)],
        compiler_params=_cparams("parallel", "parallel", "arbitrary"),
        name="nsa_attention",
    )(nq, kc, vct, nk, svt, wvt, gates_t, slopes, ovt)


def _peer_cand_layout():
    K = PEER_TOPK
    groups = [((0, 1), (0, K))]
    groups += [((j1, j1 + 1), (0, SUBLANES)) for j1 in range(1, SUBLANES)]
    groups += [((SUBLANES, K), (0, 1))]
    flat = []
    for (a0, a1), (b0, b1) in groups:
        flat += [j1 * K + j2 for j1 in range(a0, a1) for j2 in range(b0, b1)]
    return groups, np.asarray(flat, np.int32)


def _extract_top(vals, tags, n, write):
    big = jnp.int32(2 ** 30)
    for it in range(n):
        m = jnp.max(vals, axis=0, keepdims=True)
        tag = jnp.min(jnp.where(vals == m, tags, big), axis=0, keepdims=True)
        write(it, m, tag)
        vals = jnp.where(tags == tag, -jnp.inf, vals)


def _oddeven_merge_sort_pairs(n):
    pairs = []

    def merge(lo, hi, r):
        step = r * 2
        if step < hi - lo:
            merge(lo, hi, step)
            merge(lo + r, hi, step)
            pairs.extend((i, i + r) for i in range(lo + r, hi - r, step))
        else:
            pairs.append((lo, lo + r))

    def sort(lo, hi):
        if hi - lo >= 1:
            mid = lo + (hi - lo) // 2
            sort(lo, mid)
            sort(mid + 1, hi)
            merge(lo, hi, 1)

    sort(0, n - 1)
    return pairs


def _compare_exchange(v, ix, a, b):
    first = v[a] >= v[b]
    v[a], v[b] = jnp.maximum(v[a], v[b]), jnp.minimum(v[a], v[b])
    ix[a], ix[b] = jnp.where(first, ix[a], ix[b]), jnp.where(first, ix[b], ix[a])


def _sorted_top16_network(scores):
    n = PEER_TOPK
    t = scores.shape[1]
    orig = [scores[r * SUBLANES:(r + 1) * SUBLANES, :] for r in range(n)]
    v = list(orig)
    sub = lax.broadcasted_iota(jnp.int32, (SUBLANES, t), 0)
    ix = [sub + r * SUBLANES for r in range(n)]
    for a, b in _oddeven_merge_sort_pairs(n):
        _compare_exchange(v, ix, a, b)
    shift = SUBLANES // 2
    while shift >= 1:
        bv = [pltpu.roll(x, shift, 0) for x in v]
        bi = [pltpu.roll(x, shift, 0) for x in ix]
        for r in range(n):
            o = n - 1 - r
            first = v[r] >= bv[o]
            ix[r] = jnp.where(first, ix[r], bi[o])
            v[r] = jnp.maximum(v[r], bv[o])
        d = n // 2
        while d >= 1:
            for r in range(n):
                if r & d == 0:
                    _compare_exchange(v, ix, r, r + d)
            d //= 2
        shift //= 2
    strict = v[0] > v[1]
    for r in range(1, n - 1):
        strict = strict & (v[r] > v[r + 1])
    at_least = jnp.zeros((SUBLANES, t), jnp.int32)
    for r in range(n):
        at_least = at_least + (orig[r] >= v[n - 1]).astype(jnp.int32)
    sure = strict[0:1, :] & (jnp.sum(at_least, axis=0, keepdims=True) == n)
    return [x[0:1, :] for x in v], [x[0:1, :] for x in ix], sure


def _peer_topk_kernel(q_ref, keys_ref, flat_ref, a_ref, b_ref, g_ref,
                      s1_ref, s2_ref, i1_ref, i2_ref, bs_ref, bid_ref, at_ref, bt_ref, gt_ref, *, tm, groups):
    K = PEER_TOPK
    kidx = lax.broadcasted_iota(jnp.int32, (PEER_NKEYS, tm), 0)
    flat = flat_ref[...]
    for h in range(PEER_HEADS):
        halves = []
        for c, (s_ref, i_ref) in enumerate(((s1_ref, i1_ref), (s2_ref, i2_ref))):
            blk = (2 * h + c) * LANES
            scores = _dot_nt(keys_ref[h, c], q_ref[:, blk:blk + LANES])

            def write(it, m, tag, s_ref=s_ref, i_ref=i_ref):
                s_ref[it:it + 1, :] = m
                i_ref[it:it + 1, :] = tag

            vals, idxs, sure = _sorted_top16_network(scores)
            for it in range(K):
                write(it, vals[it], idxs[it])
            halves.append((scores, write, sure))
        unsure = jnp.max(jnp.where(halves[0][2] & halves[1][2], 0, 1))

        @pl.when(unsure > 0)
        def _(halves=halves):
            for scores, write, _ in halves:
                _extract_top(scores, kidx, K, write)
        s1, s2, i1, i2 = s1_ref[...], s2_ref[...], i1_ref[...], i2_ref[...]
        cand, ids = [], []
        for (a0, a1), (b0, b1) in groups:
            cand.append(s1[a0:a1] + s2[b0:b1])
            ids.append(i1[a0:a1] * PEER_NKEYS + i2[b0:b1])
        cand = jnp.concatenate(cand, axis=0)
        tags = flat * (PEER_NKEYS * PEER_NKEYS) + jnp.concatenate(ids, axis=0)

        def write2(it, m, tag):
            bs_ref[it:it + 1, :] = m
            bid_ref[it:it + 1, :] = tag

        _extract_top(cand, tags, K, write2)
        bs = bs_ref[...]
        e = jnp.exp(bs - jnp.max(bs, axis=0, keepdims=True))
        gt_ref[h * K:(h + 1) * K, :] = e / jnp.sum(e, axis=0, keepdims=True)
        bid = bid_ref[...]
        at_ref[h * K:(h + 1) * K, :] = (bid // PEER_NKEYS) % PEER_NKEYS
        bt_ref[h * K:(h + 1) * K, :] = bid % PEER_NKEYS
    for blk in range(tm // LANES):
        sl = slice(blk * LANES, (blk + 1) * LANES)
        a_ref[sl, :] = at_ref[:, sl].T
        b_ref[sl, :] = bt_ref[:, sl].T
        g_ref[sl, :] = gt_ref[:, sl].T


def peer_topk(q_bf16, sub_keys, tm=256):
    n = q_bf16.shape[0]
    groups, flat = _peer_cand_layout()
    nc = flat.shape[0]
    flat_b = jnp.asarray(np.broadcast_to(flat[:, None], (nc, tm)))
    hk = PEER_HEADS * PEER_TOPK
    row = lambda dt: jax.ShapeDtypeStruct((n, hk), dt)
    return pl.pallas_call(
        functools.partial(_peer_topk_kernel, tm=tm, groups=groups),
        out_shape=[row(jnp.int32), row(jnp.int32), row(F32)],
        grid=(n // tm,),
        in_specs=[pl.BlockSpec((tm, q_bf16.shape[1]), lambda i: (i, 0)),
                  pl.BlockSpec(sub_keys.shape, lambda i: (0, 0, 0, 0)),
                  pl.BlockSpec((nc, tm), lambda i: (0, 0))],
        out_specs=[pl.BlockSpec((tm, hk), lambda i: (i, 0))] * 3,
        scratch_shapes=[pltpu.VMEM((PEER_TOPK, tm), F32), pltpu.VMEM((PEER_TOPK, tm), F32),
                        pltpu.VMEM((PEER_TOPK, tm), jnp.int32), pltpu.VMEM((PEER_TOPK, tm), jnp.int32),
                        pltpu.VMEM((PEER_TOPK, tm), F32), pltpu.VMEM((PEER_TOPK, tm), jnp.int32),
                        pltpu.VMEM((hk, tm), jnp.int32), pltpu.VMEM((hk, tm), jnp.int32),
                        pltpu.VMEM((hk, tm), F32)],
        compiler_params=_cparams("parallel"),
        name="peer_topk",
    )(q_bf16, sub_keys.astype(BF16), flat_b)


_W_ROWS = PEER_NKEYS
_W_PITCH = _W_ROWS + SUBLANES


def _peer_apply_kernel(x_ref, ut_ref, v_ref, a_ref, b_ref, g_ref, r_ref, o_ref, w_ref, *, tm, ec, unroll):
    c = pl.program_id(1)
    slabs = ec // PEER_NKEYS
    steps = _W_ROWS // slabs
    half = c // steps
    cc = c % steps

    @pl.when(c == 0)
    def _():
        o_ref[...] = r_ref[...]

    @pl.when(cc == 0)
    def _():
        sub_a = half * _W_ROWS + lax.broadcasted_iota(jnp.int32, (_W_ROWS, LANES), 0)
        sub_b = lax.broadcasted_iota(jnp.int32, (PEER_NKEYS, LANES), 0)

        def build(tb, carry):
            for tt in range(unroll):
                t = tb * unroll + tt
                a = a_ref[pl.ds(t, 1), :]
                b = b_ref[pl.ds(t, 1), :]
                g = g_ref[pl.ds(t, 1), :]
                pt = jnp.where(sub_a == a, g, 0.0).astype(BF16)
                qt = jnp.where(sub_b == b, 1.0, 0.0).astype(BF16)
                w_ref[pl.ds(pl.multiple_of(t * _W_PITCH, SUBLANES), _W_ROWS), :] = _dot_nt(pt, qt)
            return carry

        lax.fori_loop(0, tm // unroll, build, 0)

    act = _dot(x_ref[...], ut_ref[...])
    w = jnp.concatenate([w_ref[pl.ds(cc * slabs + s, tm, stride=_W_PITCH), :] for s in range(slabs)], axis=1)
    o_ref[...] += _dot((_gelu(act) * w).astype(BF16), v_ref[...])


def peer_apply(xn_bf16, ut_bf16, v_bf16, a_idx, b_idx, gw, resid, tm=256, ec=2048, unroll=32):
    n, d = xn_bf16.shape
    ne = v_bf16.shape[0]
    assert ne == PEER_NKEYS * PEER_NKEYS and _W_ROWS % (ec // PEER_NKEYS) == 0
    tok = lambda w: pl.BlockSpec((tm, w), lambda i, c: (i, 0))
    return pl.pallas_call(
        functools.partial(_peer_apply_kernel, tm=tm, ec=ec, unroll=unroll),
        out_shape=jax.ShapeDtypeStruct((n, d), F32),
        grid=(n // tm, ne // ec),
        in_specs=[tok(d),
                  pl.BlockSpec((d, ec), lambda i, c: (0, c)),
                  pl.BlockSpec((ec, d), lambda i, c: (c, 0)),
                  tok(a_idx.shape[1]), tok(b_idx.shape[1]), tok(gw.shape[1]), tok(d)],
        out_specs=tok(d),
        scratch_shapes=[pltpu.VMEM((tm * _W_PITCH, LANES), F32)],
        compiler_params=pltpu.CompilerParams(dimension_semantics=("parallel", "arbitrary"),
                                             vmem_limit_bytes=56 * 1024 * 1024),
        name="peer_apply",
    )(xn_bf16, ut_bf16, v_bf16, a_idx, b_idx, gw, resid)


def _odd_mixer_kernel(h_ref, g_ref, win_ref, cw_ref, cb_ref, wra_ref, bra_ref, wri_ref, bri_ref, lam_ref, wout_ref,
                      o_ref, xbuf_ref, carry_ref, *, tc):
    j = pl.program_id(1)
    d_rnn = cw_ref.shape[1]
    pad = SUBLANES

    @pl.when(j == 0)
    def _():
        xbuf_ref[0:pad, :] = jnp.zeros((pad, d_rnn), F32)
        carry_ref[...] = jnp.zeros(carry_ref.shape, F32)

    h_in = h_ref[...]
    z = _dot(_rmsnorm_rows(h_in, g_ref[...]).astype(BF16), win_ref[...])
    gate = z[:, :d_rnn]
    xbuf_ref[pad:pad + tc, :] = z[:, d_rnn:]
    xc = cb_ref[...] + cw_ref[0:1, :] * xbuf_ref[pl.ds(pad - 3, tc), :]
    for w in range(1, CONV_W):
        xc = xc + cw_ref[w:w + 1, :] * xbuf_ref[pl.ds(pad - 3 + w, tc), :]
    xbuf_ref[0:pad, :] = xbuf_ref[tc:tc + pad, :]

    xcb = xc.astype(BF16)
    nb = d_rnn // RNN_BLOCK_W
    blk = lambda w_ref: jnp.concatenate(
        [_dot(xcb[:, n * RNN_BLOCK_W:(n + 1) * RNN_BLOCK_W], w_ref[n]) for n in range(nb)], axis=1)
    r = _sigmoid(blk(wra_ref) + bra_ref[...])
    gi = _sigmoid(blk(wri_ref) + bri_ref[...])
    nl = -lam_ref[...]
    softplus = jnp.maximum(nl, 0.0) + jnp.log(1.0 + jnp.exp(-jnp.abs(nl)))
    log_a = (-LRU_C) * r * softplus
    a = jnp.exp(log_a)
    u = jnp.sqrt(1.0 - jnp.exp(2.0 * log_a)) * (gi * xc)

    row = lax.broadcasted_iota(jnp.int32, (tc, d_rnn), 0)
    shift = 1
    while shift < tc:
        valid = row >= shift
        a_prev = jnp.where(valid, pltpu.roll(a, shift, 0), 1.0)
        u_prev = jnp.where(valid, pltpu.roll(u, shift, 0), 0.0)
        u = u + a * u_prev
        a = a * a_prev
        shift *= 2
    hs = u + a * carry_ref[...]
    carry_ref[...] = hs[tc - 1:tc, :]

    y = (_gelu(gate) * hs).astype(BF16)
    o_ref[...] = h_in + _dot(y, wout_ref[...])


def odd_mixer_residual(h, norm_g, w_in, conv_w, conv_b, w_ra, b_ra, w_ri, b_ri, lam, w_out, tc=256):
    b, s, d = h.shape
    d_rnn = conv_w.shape[1]
    full = lambda a: pl.BlockSpec(a.shape, lambda bb, j, nd=a.ndim: (0,) * nd)
    row = lambda v: v.astype(F32).reshape(1, -1)
    args = [norm_g.astype(F32).reshape(1, d), w_in.astype(BF16), conv_w.astype(F32), row(conv_b),
            w_ra.astype(BF16), row(b_ra), w_ri.astype(BF16), row(b_ri), row(lam), w_out.astype(BF16)]
    return pl.pallas_call(
        functools.partial(_odd_mixer_kernel, tc=tc),
        out_shape=jax.ShapeDtypeStruct((b, s, d), F32),
        grid=(b, s // tc),
        in_specs=[pl.BlockSpec((None, tc, d), lambda bb, j: (bb, j, 0))] + [full(a) for a in args],
        out_specs=pl.BlockSpec((None, tc, d), lambda bb, j: (bb, j, 0)),
        scratch_shapes=[pltpu.VMEM((tc + 2 * SUBLANES, d_rnn), F32), pltpu.VMEM((1, d_rnn), F32)],
        compiler_params=_cparams("parallel", "arbitrary"),
        name="odd_mixer",
    )(h, *args)


def _rmsnorm_kernel(x_ref, g_ref, o_ref):
    o_ref[...] = _rmsnorm_rows(x_ref[...], g_ref[...])


def rmsnorm_rows(x2d, g, tm=512):
    n, d = x2d.shape
    return pl.pallas_call(
        _rmsnorm_kernel,
        out_shape=jax.ShapeDtypeStruct((n, d), F32),
        grid=(n // tm,),
        in_specs=[pl.BlockSpec((tm, d), lambda i: (i, 0)), pl.BlockSpec((1, d), lambda i: (0, 0))],
        out_specs=pl.BlockSpec((tm, d), lambda i: (i, 0)),
        compiler_params=_cparams("parallel"),
        name="final_rmsnorm",
    )(x2d, g.astype(F32).reshape(1, d))


def even_mixer_residual(h, norm_g, w_in, b_f, cmp_pos, w_cmp_k, w_cmp_v, w_out):
    b, s, d = h.shape
    h2d = h.reshape(b * s, d)
    w_ext, wt_ext = _even_w_ext(w_in)
    outs = norm_proj(h2d, norm_g.astype(F32), w_ext, _EVEN_SEGS, _EVEN_DTYPES, wt_ext, _EVEN_TSEGS, name="even_in_proj")
    fq, fk, nq, ncmp, nk, small = [o.reshape(b, s, -1) for o in outs[:6]]
    fvt, svt, wvt = outs[6:]
    c = fox_gate_cumsum(small[..., :N_FOX_HEADS].transpose(0, 2, 1), b_f)
    o_fox = fox_attention(fq, fk, fvt, c)
    kc, vct = nsa_compress(ncmp, cmp_pos, w_cmp_k, w_cmp_v)
    gates_t = small[..., N_FOX_HEADS:N_FOX_HEADS + NSA_GATES].reshape(b, s, N_NSA_KV, 3 * NSA_GROUP).transpose(0, 2, 3, 1)
    gates_t = jnp.pad(gates_t, ((0, 0), (0, 0), (0, 2 * SUBLANES - 3 * NSA_GROUP), (0, 0)))
    o_nsa = nsa_attention(nq, kc, vct, nk, svt, wvt, gates_t)
    nf = N_FOX_HEADS * HEAD_DIM
    w_n = w_out[nf:].reshape(N_NSA_HEADS, HEAD_DIM, d)
    half_n = (np.arange(N_NSA_HEADS) // NSA_GROUP)[:, None, None, None] == np.arange(2)[None, :, None, None]
    w_n = jnp.where(half_n, w_n[:, None], 0.0).reshape(N_NSA_HEADS * LANES, d)
    pairs = [(o_fox.reshape(b * s, -1), w_out[:nf].astype(BF16)), (o_nsa.reshape(b * s, -1), w_n.astype(BF16))]
    return proj_residual(pairs, h2d, name="even_out_proj").reshape(b, s, d)


def peer_ffn_residual(h2d, norm_g, w_q, sub_keys, u_tab, v_tab):
    d = h2d.shape[1]
    nq = w_q.shape[1]
    q, xn = norm_proj(h2d, norm_g, w_q.astype(BF16), ((0, nq, 1.0), (None, d, 1.0)), (BF16, BF16), name="peer_query")
    a_idx, b_idx, gw = peer_topk(q, sub_keys)
    return peer_apply(xn, u_tab.astype(BF16).T, v_tab.astype(BF16), a_idx, b_idx, gw, h2d)


def kernel(x, even_norm_g, even_w_in, even_b_f, even_cmp_pos, even_w_cmp_k, even_w_cmp_v, even_w_out, odd_norm_g, odd_w_in, odd_conv_w, odd_conv_b, odd_w_ra, odd_b_ra, odd_w_ri, odd_b_ri, odd_lam, odd_w_out, ffn_norm_g, peer_w_q, peer_sub_keys, peer_u, peer_v, final_g):
    b, s, d = x.shape
    depth = ffn_norm_g.shape[0]
    h = x
    for layer in range(depth):
        j = layer // 2
        if layer % 2 == 0:
            h = even_mixer_residual(h, even_norm_g[j], even_w_in[j], even_b_f[j], even_cmp_pos[j],
                                    even_w_cmp_k[j], even_w_cmp_v[j], even_w_out[j])
        else:
            h = odd_mixer_residual(h, odd_norm_g[j], odd_w_in[j], odd_conv_w[j], odd_conv_b[j], odd_w_ra[j],
                                   odd_b_ra[j], odd_w_ri[j], odd_b_ri[j], odd_lam[j], odd_w_out[j])
        h = peer_ffn_residual(h.reshape(b * s, d), ffn_norm_g[layer].astype(F32), peer_w_q[layer],
                              peer_sub_keys[layer], peer_u[layer], peer_v[layer]).reshape(b, s, d)
    return rmsnorm_rows(h.reshape(b * s, d), final_g).reshape(b, s, d)
```

```python
import functools
import math

import jax
import jax.numpy as jnp
import numpy as np
from jax import lax
from jax.experimental import pallas as pl
from jax.experimental.pallas import tpu as pltpu

HEAD_DIM = 64
N_FOX_HEADS = 8
N_NSA_HEADS = 8
N_NSA_KV = 2
NSA_GROUP = N_NSA_HEADS // N_NSA_KV
CMP_LEN = 32
CMP_STRIDE = 16
SLC_LEN = 64
SLC_TOPK = 16
WINDOW = 512
FORCE_BONUS = 1.0e4
NEG = -1.0e30
D_RNN = 1280
RNN_BLOCKS = 10
RNN_BLOCK_W = D_RNN // RNN_BLOCKS
CONV_W = 4
LRU_C = 8.0
PEER_HEADS = 8
PEER_NKEYS = 128
PEER_HALF = 128
PEER_TOPK = 16
RMS_EPS = 1e-6

LANES = 128
SUBLANES = 8
VMEM_LIMIT = 48 * 1024 * 1024

F32 = jnp.float32
BF16 = jnp.bfloat16


def _cparams(*sem):
    return pltpu.CompilerParams(dimension_semantics=sem, vmem_limit_bytes=VMEM_LIMIT)


def _dot(a, b):
    return lax.dot_general(a, b, (((1,), (0,)), ((), ())), preferred_element_type=F32)


def _dot_nt(a, b):
    return lax.dot_general(a, b, (((1,), (1,)), ((), ())), preferred_element_type=F32)


def _rmsnorm_rows(x, g):
    return x * lax.rsqrt(jnp.mean(x * x, axis=-1, keepdims=True) + RMS_EPS) * g


def _gelu(x):
    return 0.5 * x * (1.0 + lax.erf(x * (1.0 / math.sqrt(2.0))))


def _sigmoid(x):
    return 1.0 / (1.0 + jnp.exp(-x))


def _norm_proj_kernel(x_ref, g_ref, w_ref, wt_ref, *out_refs, segs, tsegs):
    xn = _rmsnorm_rows(x_ref[...], g_ref[...]).astype(BF16)
    for o_ref, (start, width, scale) in zip(out_refs, segs):
        if start is None:
            o_ref[...] = xn
            continue
        y = _dot(xn, w_ref[:, start:start + width])
        if scale != 1.0:
            y = y * scale
        o_ref[...] = y.astype(o_ref.dtype)
    for o_ref, (start, width) in zip(out_refs[len(segs):], tsegs):
        o_ref[...] = _dot_nt(wt_ref[start:start + width, :], xn).astype(o_ref.dtype)


def norm_proj(x2d, g, w_bf16, segs, dtypes, wt_bf16=None, tsegs=(), tm=256, name="norm_proj"):
    n, d = x2d.shape
    if wt_bf16 is None:
        wt_bf16 = jnp.zeros((SUBLANES, d), BF16)
    outs = [jax.ShapeDtypeStruct((n, wd), dt) for (_, wd, _), dt in zip(segs, dtypes)]
    outs += [jax.ShapeDtypeStruct((wd, n), BF16) for (_, wd) in tsegs]
    return pl.pallas_call(
        functools.partial(_norm_proj_kernel, segs=tuple(segs), tsegs=tuple(tsegs)),
        out_shape=outs,
        grid=(n // tm,),
        in_specs=[pl.BlockSpec((tm, d), lambda i: (i, 0)),
                  pl.BlockSpec((1, d), lambda i: (0, 0)),
                  pl.BlockSpec(w_bf16.shape, lambda i: (0, 0)),
                  pl.BlockSpec(wt_bf16.shape, lambda i: (0, 0))],
        out_specs=([pl.BlockSpec((tm, wd), lambda i: (i, 0)) for (_, wd, _) in segs]
                   + [pl.BlockSpec((wd, tm), lambda i: (0, i)) for (_, wd) in tsegs]),
        compiler_params=_cparams("parallel"),
        name=name,
    )(x2d, g.reshape(1, d), w_bf16, wt_bf16)


def _proj_residual_kernel(*refs):
    r_ref, o_ref = refs[-2], refs[-1]
    acc = r_ref[...]
    for a_ref, w_ref in zip(refs[0:-2:2], refs[1:-2:2]):
        acc = acc + _dot(a_ref[...], w_ref[...])
    o_ref[...] = acc


def proj_residual(pairs, resid, tm=512, name="proj_residual"):
    n, d = resid.shape
    args, specs = [], []
    for a, w in pairs:
        args += [a, w]
        specs += [pl.BlockSpec((tm, a.shape[1]), lambda i: (i, 0)), pl.BlockSpec(w.shape, lambda i: (0, 0))]
    return pl.pallas_call(
        _proj_residual_kernel,
        out_shape=jax.ShapeDtypeStruct((n, d), F32),
        grid=(n // tm,),
        in_specs=specs + [pl.BlockSpec((tm, d), lambda i: (i, 0))],
        out_specs=pl.BlockSpec((tm, d), lambda i: (i, 0)),
        compiler_params=_cparams("parallel"),
        name=name,
    )(*args, resid)


def _fox_gate_kernel(f_ref, b_ref, c_ref):
    z = f_ref[...] + b_ref[...]
    x = jnp.minimum(z, 0.0) - jnp.log(1.0 + jnp.exp(-jnp.abs(z)))
    s = x.shape[-1]
    lane = lax.broadcasted_iota(jnp.int32, x.shape, 1)
    shift = 1
    while shift < s:
        x = x + jnp.where(lane >= shift, pltpu.roll(x, shift, 1), 0.0)
        shift *= 2
    c_ref[...] = x


def fox_gate_cumsum(f_t, b_f):
    b, h, s = f_t.shape
    return pl.pallas_call(
        _fox_gate_kernel,
        out_shape=jax.ShapeDtypeStruct((b, h, s), F32),
        grid=(b,),
        in_specs=[pl.BlockSpec((None, h, s), lambda i: (i, 0, 0)),
                  pl.BlockSpec((h, 1), lambda i: (0, 0))],
        out_specs=pl.BlockSpec((None, h, s), lambda i: (i, 0, 0)),
        compiler_params=_cparams("parallel"),
        name="fox_gate_cumsum",
    )(f_t, b_f.reshape(h, 1).astype(F32))


ATTN_KB = 512
FOX_KB = 1024


def _fox_attn_kernel(q_ref, k_ref, vt_ref, ct_ref, cs_ref, o_ref, qs_ref, m_ref, l_ref, acc_ref, *, tq):
    KB = min(FOX_KB, k_ref.shape[0])
    i = pl.program_id(2)
    t0 = i * tq
    for hh in range(2):
        qs_ref[hh * tq:(hh + 1) * tq, :] = q_ref[:, hh * LANES:(hh + 1) * LANES]
    qs = qs_ref[...]
    m_ref[...] = jnp.full(m_ref.shape, -jnp.inf, F32)
    l_ref[...] = jnp.zeros(l_ref.shape, F32)
    acc_ref[...] = jnp.zeros(acc_ref.shape, F32)
    t_pos = t0 + lax.broadcasted_iota(jnp.int32, (KB, tq), 1)
    s_in = lax.broadcasted_iota(jnp.int32, (KB, tq), 0)

    def step(kb, masked):
        start = pl.multiple_of(kb * KB, KB)
        st = _dot_nt(k_ref[pl.ds(start, KB), :], qs)
        vt = vt_ref[:, pl.ds(start, KB)]
        cs = cs_ref[pl.ds(start, KB), :]
        for hh in range(2):
            s = st[:, hh * tq:(hh + 1) * tq] + (ct_ref[hh:hh + 1, :] - cs[:, hh:hh + 1])
            if masked:
                s = jnp.where(start + s_in <= t_pos, s, NEG)
            m_old = m_ref[hh:hh + 1, :]
            m_new = jnp.maximum(m_old, jnp.max(s, axis=0, keepdims=True))
            alpha = jnp.exp(m_old - m_new)
            p = jnp.exp(s - m_new)
            l_ref[hh:hh + 1, :] = alpha * l_ref[hh:hh + 1, :] + jnp.sum(p, axis=0, keepdims=True)
            acc_ref[hh] = alpha * acc_ref[hh] + _dot(vt, p.astype(BF16))
            m_ref[hh:hh + 1, :] = m_new

    n_full = t0 // KB

    def full_body(kb, carry):
        step(kb, False)
        return carry

    lax.fori_loop(0, n_full, full_body, 0)
    step(n_full, True)

    row = lax.broadcasted_iota(jnp.int32, (LANES, tq), 0)
    o0 = acc_ref[0] / l_ref[0:1, :]
    o1 = acc_ref[1] / l_ref[1:2, :]
    ot = jnp.where(row < HEAD_DIM, o0, o1)
    for blk in range(tq // LANES):
        o_ref[blk * LANES:(blk + 1) * LANES, :] = ot[:, blk * LANES:(blk + 1) * LANES].T.astype(o_ref.dtype)


def fox_attention(fq, fk, fvt, c, tq=512):
    b, s, _ = fk.shape
    assert min(FOX_KB, s) % tq == 0 and s % min(FOX_KB, s) == 0
    hp = N_FOX_HEADS // 2
    ct = c.reshape(b, hp, 2, s)
    cs = ct.transpose(0, 1, 3, 2)
    return pl.pallas_call(
        functools.partial(_fox_attn_kernel, tq=tq),
        out_shape=jax.ShapeDtypeStruct((b, s, N_FOX_HEADS * HEAD_DIM), BF16),
        grid=(b, hp, s // tq),
        in_specs=[pl.BlockSpec((None, tq, 2 * LANES), lambda bb, p, i: (bb, i, p)),
                  pl.BlockSpec((None, s, LANES), lambda bb, p, i: (bb, 0, p)),
                  pl.BlockSpec((LANES, s), lambda bb, p, i: (p, bb)),
                  pl.BlockSpec((None, None, 2, tq), lambda bb, p, i: (bb, p, 0, i)),
                  pl.BlockSpec((None, None, s, 2), lambda bb, p, i: (bb, p, 0, 0))],
        out_specs=pl.BlockSpec((None, tq, LANES), lambda bb, p, i: (bb, i, p)),
        scratch_shapes=[pltpu.VMEM((2 * tq, LANES), BF16), pltpu.VMEM((2, tq), F32), pltpu.VMEM((2, tq), F32),
                        pltpu.VMEM((2, LANES, tq), F32)],
        compiler_params=_cparams("parallel", "parallel", "arbitrary"),
        name="fox_attention",
    )(fq, fk, fvt, ct, cs)


FOX_QKV = 3 * N_FOX_HEADS * HEAD_DIM
NSA_Q = N_NSA_HEADS * HEAD_DIM
NSA_KV = 6 * N_NSA_KV * HEAD_DIM
NSA_GATES = 3 * N_NSA_HEADS
_QK_SCALE = HEAD_DIM ** -0.5

_EVEN_SEGS = ((0, 1024, _QK_SCALE), (1024, 512, 1.0), (1536, 1024, _QK_SCALE),
              (2560, 256, 1.0), (2816, 256, 1.0), (3072, 128, 1.0))
_EVEN_DTYPES = (BF16, BF16, BF16, F32, BF16, F32)
_EVEN_TSEGS = ((0, 512), (512, 128), (640, 128))


def _even_w_ext(w_in):
    d = w_in.shape[0]
    o = 0
    w_fq = w_in[:, o:o + 512].reshape(d, N_FOX_HEADS, HEAD_DIM); o += 512
    w_fk = w_in[:, o:o + 512]; o += 512
    w_fv = w_in[:, o:o + 512]; o += 512
    w_ff = w_in[:, o:o + N_FOX_HEADS]; o += N_FOX_HEADS
    w_nq = w_in[:, o:o + NSA_Q].reshape(d, N_NSA_HEADS, HEAD_DIM); o += NSA_Q
    w_nkv = w_in[:, o:o + NSA_KV]; o += NSA_KV
    w_ng = w_in[:, o:o + NSA_GATES]
    half_f = (np.arange(N_FOX_HEADS) % 2)[None, :, None, None] == np.arange(2)[None, None, :, None]
    fq = jnp.where(half_f, w_fq[:, :, None, :], 0.0).reshape(d, N_FOX_HEADS * LANES)
    half_n = (np.arange(N_NSA_HEADS) // NSA_GROUP)[None, :, None, None] == np.arange(2)[None, None, :, None]
    nq = jnp.where(half_n, w_nq[:, :, None, :], 0.0).reshape(d, N_NSA_HEADS * LANES)
    small = jnp.concatenate([w_ff, w_ng, jnp.zeros((d, LANES - N_FOX_HEADS - NSA_GATES), w_in.dtype)], axis=1)
    w = jnp.concatenate([fq, w_fk, nq, w_nkv[:, 0:256], w_nkv[:, 256:384], w_nkv[:, 512:640], small], axis=1)
    wt = jnp.concatenate([w_fv, w_nkv[:, 384:512], w_nkv[:, 640:768]], axis=1).T
    return w.astype(BF16), wt.astype(BF16)


def _nsa_compress_kernel(x_ref, pos_ref, w_ref, kc_ref, vct_ref):
    n = x_ref.shape[2]
    for kind in range(2):
        acc = jnp.zeros((n, LANES), F32)
        for g in range(N_NSA_KV):
            x = x_ref[kind, g]
            top = _dot((x + pos_ref[0:1, :]).astype(BF16), w_ref[kind, g, 0])
            bot = _dot((x + pos_ref[1:2, :]).astype(BF16), w_ref[kind, g, 1])
            acc = acc + top + pltpu.roll(bot, n - 1, 0)
        if kind == 0:
            kc_ref[...] = acc.astype(kc_ref.dtype)
        else:
            for blk in range(n // LANES):
                sl = slice(blk * LANES, (blk + 1) * LANES)
                vct_ref[:, sl] = acc[sl, :].T.astype(vct_ref.dtype)


def nsa_compress(ncmp, cmp_pos, w_cmp_k, w_cmp_v):
    b, s, _ = ncmp.shape
    n = s // CMP_STRIDE
    half = CMP_STRIDE * HEAD_DIM
    x = ncmp.reshape(b, n, CMP_STRIDE, 2, N_NSA_KV, HEAD_DIM).transpose(0, 3, 4, 1, 2, 5).reshape(b, 2, N_NSA_KV, n, half)
    pos = cmp_pos.astype(F32).reshape(2, half)
    w = jnp.stack([w_cmp_k, w_cmp_v]).astype(F32).reshape(2, 2, half, HEAD_DIM)
    place = (np.arange(N_NSA_KV)[:, None, None] == np.arange(2)[None, :, None])
    w_pad = jnp.where(place[None, :, None, None], w[:, None, :, :, None, :], 0.0)
    w_pad = w_pad.reshape(2, N_NSA_KV, 2, half, LANES).astype(BF16)
    return pl.pallas_call(
        _nsa_compress_kernel,
        out_shape=[jax.ShapeDtypeStruct((b, n, LANES), BF16), jax.ShapeDtypeStruct((b, LANES, n), BF16)],
        grid=(b,),
        in_specs=[pl.BlockSpec((None, 2, N_NSA_KV, n, half), lambda i: (i, 0, 0, 0, 0)),
                  pl.BlockSpec((2, half), lambda i: (0, 0)),
                  pl.BlockSpec((2, N_NSA_KV, 2, half, LANES), lambda i: (0, 0, 0, 0, 0))],
        out_specs=[pl.BlockSpec((None, n, LANES), lambda i: (i, 0, 0)),
                   pl.BlockSpec((None, LANES, n), lambda i: (i, 0, 0))],
        compiler_params=_cparams("parallel"),
        name="nsa_compress",
    )(x, pos, w_pad)


def _nsa_attn_kernel(q_ref, kc_ref, vct_ref, nk_ref, svt_ref, wvt_ref, gate_ref, slope_ref, ovt_ref, o_ref,
                     q4_ref, m_ref, l_ref, acc_ref, selt_ref, *, tq):
    g = pl.program_id(1)
    i = pl.program_id(2)
    t0 = i * tq
    n_cmp = kc_ref.shape[0]
    R = NSA_GROUP

    for r in range(R):
        q4_ref[r * tq:(r + 1) * tq, :] = q_ref[:, r * LANES:(r + 1) * LANES]
    q4 = q4_ref[...]

    t_c = t0 + lax.broadcasted_iota(jnp.int32, (n_cmp, tq), 1)
    cend = CMP_STRIDE * lax.broadcasted_iota(jnp.int32, (n_cmp, tq), 0) + (CMP_LEN - 1)
    cmask = cend <= t_c
    cdist = (t_c - cend).astype(F32)
    any_c = (t_c[0:1, :] >= CMP_LEN - 1).astype(F32)
    s_c = _dot_nt(kc_ref[...], q4)
    imp_t = jnp.zeros((ovt_ref.shape[0], tq), F32)
    o_cmp = []
    for r in range(R):
        s = s_c[:, r * tq:(r + 1) * tq] - slope_ref[r:r + 1, 0:1] * cdist
        s = jnp.where(cmask, s, NEG)
        e = jnp.exp(s - jnp.max(s, axis=0, keepdims=True))
        p = (e / jnp.sum(e, axis=0, keepdims=True)) * any_c
        pb = p.astype(BF16)
        o_cmp.append(_dot(vct_ref[...], pb))
        imp_t = imp_t + _dot(ovt_ref[...], pb)

    n_sel = ovt_ref.shape[0]
    jrow = lax.broadcasted_iota(jnp.int32, (n_sel, tq), 0)
    cur = (t0 + lax.broadcasted_iota(jnp.int32, (n_sel, tq), 1)) // SLC_LEN
    forced = ((jrow == 0) | (jrow == cur) | (jrow == cur - 1)).astype(F32)
    imp_t = jnp.where(jrow <= cur, imp_t + FORCE_BONUS * forced, -jnp.inf)
    ngrp = n_sel // SUBLANES
    grp = [imp_t[gi * SUBLANES:(gi + 1) * SUBLANES, :] for gi in range(ngrp)]
    sub8 = lax.broadcasted_iota(jnp.int32, (SUBLANES, tq), 0)
    rank = [jnp.zeros((SUBLANES, tq), jnp.int32) for _ in range(ngrp)]
    for jp in range(n_sel):
        other = imp_t[jp:jp + 1, :]
        for gi in range(ngrp):
            lo = gi * SUBLANES
            if lo + SUBLANES - 1 < jp:
                ahead = other > grp[gi]
            elif lo > jp:
                ahead = other >= grp[gi]
            else:
                ahead = (other > grp[gi]) | ((other == grp[gi]) & (sub8 > jp - lo))
            rank[gi] = rank[gi] + ahead.astype(jnp.int32)
    for gi in range(ngrp):
        selt_ref[gi * SUBLANES:(gi + 1) * SUBLANES, :] = (rank[gi] < SLC_TOPK).astype(F32)

    KB = ATTN_KB
    t_pos = t0 + lax.broadcasted_iota(jnp.int32, (KB, tq), 1)
    s_in = lax.broadcasted_iota(jnp.int32, (KB, tq), 0)
    per_kb = KB // SLC_LEN
    m_ref[...] = jnp.full(m_ref.shape, -jnp.inf, F32)
    l_ref[...] = jnp.zeros(l_ref.shape, F32)
    acc_ref[...] = jnp.zeros(acc_ref.shape, F32)

    def sel_body(kb, carry):
        start = pl.multiple_of(kb * KB, KB)
        dist_i = t_pos - (start + s_in)
        dist = dist_i.astype(F32)
        chosen = jnp.concatenate(
            [jnp.broadcast_to(selt_ref[pl.ds(kb * per_kb + c, 1), :], (SLC_LEN, tq)) for c in range(per_kb)], axis=0)
        mask = (dist_i >= 0) & (chosen > 0.5)
        st = _dot_nt(nk_ref[pl.ds(start, KB), 0:LANES], q4)
        vt = svt_ref[:, pl.ds(start, KB)]
        for r in range(R):
            s = st[:, r * tq:(r + 1) * tq] - slope_ref[r:r + 1, 0:1] * dist
            s = jnp.where(mask, s, NEG)
            m_old = m_ref[r:r + 1, :]
            m_new = jnp.maximum(m_old, jnp.max(s, axis=0, keepdims=True))
            alpha = jnp.exp(m_old - m_new)
            p = jnp.exp(s - m_new)
            l_ref[r:r + 1, :] = alpha * l_ref[r:r + 1, :] + jnp.sum(p, axis=0, keepdims=True)
            acc_ref[r] = alpha * acc_ref[r] + _dot(vt, p.astype(BF16))
            m_ref[r:r + 1, :] = m_new
        return carry

    lax.fori_loop(0, t0 // KB + 1, sel_body, 0)

    nwb = (WINDOW + tq) // tq
    w_first = i - (nwb - 1)
    starts = [pl.multiple_of(jnp.maximum(w_first + d, 0) * tq, tq) for d in range(nwb)]
    kw = jnp.concatenate([nk_ref[pl.ds(st_, tq), LANES:2 * LANES] for st_ in starts], axis=0)
    vtw = jnp.concatenate([wvt_ref[:, pl.ds(st_, tq)] for st_ in starts], axis=1)
    wl = nwb * tq
    key_pos = w_first * tq + lax.broadcasted_iota(jnp.int32, (wl, tq), 0)
    wdist_i = (t0 + lax.broadcasted_iota(jnp.int32, (wl, tq), 1)) - key_pos
    wmask = (wdist_i >= 0) & (wdist_i < WINDOW) & (key_pos >= 0)
    wdist = wdist_i.astype(F32)
    stw = _dot_nt(kw, q4)
    o_win = []
    for r in range(R):
        s = jnp.where(wmask, stw[:, r * tq:(r + 1) * tq] - slope_ref[r:r + 1, 0:1] * wdist, NEG)
        e = jnp.exp(s - jnp.max(s, axis=0, keepdims=True))
        p = e / jnp.sum(e, axis=0, keepdims=True)
        o_win.append(_dot(vtw, p.astype(BF16)))

    gates = _sigmoid(gate_ref[...])
    row = lax.broadcasted_iota(jnp.int32, (LANES, tq), 0)
    mine = (row // HEAD_DIM) == g
    for r in range(R):
        ot = (gates[3 * r:3 * r + 1, :] * o_cmp[r] + gates[3 * r + 1:3 * r + 2, :] * (acc_ref[r] / l_ref[r:r + 1, :])
              + gates[3 * r + 2:3 * r + 3, :] * o_win[r])
        ot = jnp.where(mine, ot, 0.0)
        for blk in range(tq // LANES):
            o_ref[blk * LANES:(blk + 1) * LANES, r * LANES:(r + 1) * LANES] = (
                ot[:, blk * LANES:(blk + 1) * LANES].T.astype(o_ref.dtype))


def _alibi_slopes(n):
    return np.asarray(2.0 ** (-8.0 * np.arange(1, n + 1) / n), np.float32)


def nsa_attention(nq, kc, vct, nk, svt, wvt, gates_t, tq=256):
    b, s, _ = nk.shape
    assert ATTN_KB % tq == 0 and s % ATTN_KB == 0 and WINDOW % tq == 0
    n_cmp = s // CMP_STRIDE
    n_sel = s // SLC_LEN
    slopes = np.broadcast_to(_alibi_slopes(N_NSA_HEADS).reshape(N_NSA_KV, NSA_GROUP, 1), (N_NSA_KV, NSA_GROUP, LANES))
    slopes = jnp.asarray(np.concatenate([slopes, np.zeros((N_NSA_KV, SUBLANES - NSA_GROUP, LANES), np.float32)], axis=1))
    cstart = CMP_STRIDE * np.arange(n_cmp)
    sstart = SLC_LEN * np.arange(n_sel)
    ov = ((cstart[None, :] < sstart[:, None] + SLC_LEN) & (cstart[None, :] + CMP_LEN > sstart[:, None])
          & (cstart[None, :] + CMP_LEN <= s))
    ovt = jnp.asarray(ov, BF16)
    return pl.pallas_call(
        functools.partial(_nsa_attn_kernel, tq=tq),
        out_shape=jax.ShapeDtypeStruct((b, s, N_NSA_HEADS * LANES), BF16),
        grid=(b, N_NSA_KV, s // tq),
        in_specs=[pl.BlockSpec((None, tq, NSA_GROUP * LANES), lambda bb, g, i: (bb, i, g)),
                  pl.BlockSpec((None, n_cmp, LANES), lambda bb, g, i: (bb, 0, 0)),
                  pl.BlockSpec((None, LANES, n_cmp), lambda bb, g, i: (bb, 0, 0)),
                  pl.BlockSpec((None, s, 2 * LANES), lambda bb, g, i: (bb, 0, 0)),
                  pl.BlockSpec((LANES, s), lambda bb, g, i: (0, bb)),
                  pl.BlockSpec((LANES, s), lambda bb, g, i: (0, bb)),
                  pl.BlockSpec((None, None, 2 * SUBLANES, tq), lambda bb, g, i: (bb, g, 0, i)),
                  pl.BlockSpec((None, SUBLANES, LANES), lambda bb, g, i: (g, 0, 0)),
                  pl.BlockSpec((n_sel, n_cmp), lambda bb, g, i: (0, 0))],
        out_specs=pl.BlockSpec((None, tq, NSA_GROUP * LANES), lambda bb, g, i: (bb, i, g)),
        scratch_shapes=[pltpu.VMEM((NSA_GROUP * tq, LANES), BF16),
                        pltpu.VMEM((NSA_GROUP, tq), F32), pltpu.VMEM((NSA_GROUP, tq), F32),
                        pltpu.VMEM((NSA_GROUP, LANES, tq), F32), pltpu.VMEM((n_sel, tq), F32)],
        compiler_params=_cparams("parallel", "parallel", "arbitrary"),
        name="nsa_attention",
    )(nq, kc, vct, nk, svt, wvt, gates_t, slopes, ovt)


def _peer_cand_layout():
    K = PEER_TOPK
    groups = [((0, 1), (0, K))]
    groups += [((j1, j1 + 1), (0, SUBLANES)) for j1 in range(1, SUBLANES)]
    groups += [((SUBLANES, K), (0, 1))]
    flat = []
    for (a0, a1), (b0, b1) in groups:
        flat += [j1 * K + j2 for j1 in range(a0, a1) for j2 in range(b0, b1)]
    return groups, np.asarray(flat, np.int32)


def _extract_top(vals, tags, n, write):
    big = jnp.int32(2 ** 30)
    for it in range(n):
        m = jnp.max(vals, axis=0, keepdims=True)
        tag = jnp.min(jnp.where(vals == m, tags, big), axis=0, keepdims=True)
        write(it, m, tag)
        vals = jnp.where(tags == tag, -jnp.inf, vals)


def _oddeven_merge_sort_pairs(n):
    pairs = []

    def merge(lo, hi, r):
        step = r * 2
        if step < hi - lo:
            merge(lo, hi, step)
            merge(lo + r, hi, step)
            pairs.extend((i, i + r) for i in range(lo + r, hi - r, step))
        else:
            pairs.append((lo, lo + r))

    def sort(lo, hi):
        if hi - lo >= 1:
            mid = lo + (hi - lo) // 2
            sort(lo, mid)
            sort(mid + 1, hi)
            merge(lo, hi, 1)

    sort(0, n - 1)
    return pairs


def _compare_exchange(v, ix, a, b):
    first = v[a] >= v[b]
    v[a], v[b] = jnp.maximum(v[a], v[b]), jnp.minimum(v[a], v[b])
    ix[a], ix[b] = jnp.where(first, ix[a], ix[b]), jnp.where(first, ix[b], ix[a])


def _sorted_top16_network(scores):
    n = PEER_TOPK
    t = scores.shape[1]
    orig = [scores[r * SUBLANES:(r + 1) * SUBLANES, :] for r in range(n)]
    v = list(orig)
    sub = lax.broadcasted_iota(jnp.int32, (SUBLANES, t), 0)
    ix = [sub + r * SUBLANES for r in range(n)]
    for a, b in _oddeven_merge_sort_pairs(n):
        _compare_exchange(v, ix, a, b)
    shift = SUBLANES // 2
    while shift >= 1:
        bv = [pltpu.roll(x, shift, 0) for x in v]
        bi = [pltpu.roll(x, shift, 0) for x in ix]
        for r in range(n):
            o = n - 1 - r
            first = v[r] >= bv[o]
            ix[r] = jnp.where(first, ix[r], bi[o])
            v[r] = jnp.maximum(v[r], bv[o])
        d = n // 2
        while d >= 1:
            for r in range(n):
                if r & d == 0:
                    _compare_exchange(v, ix, r, r + d)
            d //= 2
        shift //= 2
    strict = v[0] > v[1]
    for r in range(1, n - 1):
        strict = strict & (v[r] > v[r + 1])
    at_least = jnp.zeros((SUBLANES, t), jnp.int32)
    for r in range(n):
        at_least = at_least + (orig[r] >= v[n - 1]).astype(jnp.int32)
    sure = strict[0:1, :] & (jnp.sum(at_least, axis=0, keepdims=True) == n)
    return [x[0:1, :] for x in v], [x[0:1, :] for x in ix], sure


def _peer_topk_kernel(q_ref, keys_ref, flat_ref, a_ref, b_ref, g_ref,
                      s1_ref, s2_ref, i1_ref, i2_ref, bs_ref, bid_ref, at_ref, bt_ref, gt_ref, *, tm, groups):
    K = PEER_TOPK
    kidx = lax.broadcasted_iota(jnp.int32, (PEER_NKEYS, tm), 0)
    flat = flat_ref[...]
    for h in range(PEER_HEADS):
        halves = []
        for c, (s_ref, i_ref) in enumerate(((s1_ref, i1_ref), (s2_ref, i2_ref))):
            blk = (2 * h + c) * LANES
            scores = _dot_nt(keys_ref[h, c], q_ref[:, blk:blk + LANES])

            def write(it, m, tag, s_ref=s_ref, i_ref=i_ref):
                s_ref[it:it + 1, :] = m
                i_ref[it:it + 1, :] = tag

            vals, idxs, sure = _sorted_top16_network(scores)
            for it in range(K):
                write(it, vals[it], idxs[it])
            halves.append((scores, write, sure))
        unsure = jnp.max(jnp.where(halves[0][2] & halves[1][2], 0, 1))

        @pl.when(unsure > 0)
        def _(halves=halves):
            for scores, write, _ in halves:
                _extract_top(scores, kidx, K, write)
        s1, s2, i1, i2 = s1_ref[...], s2_ref[...], i1_ref[...], i2_ref[...]
        cand, ids = [], []
        for (a0, a1), (b0, b1) in groups:
            cand.append(s1[a0:a1] + s2[b0:b1])
            ids.append(i1[a0:a1] * PEER_NKEYS + i2[b0:b1])
        cand = jnp.concatenate(cand, axis=0)
        tags = flat * (PEER_NKEYS * PEER_NKEYS) + jnp.concatenate(ids, axis=0)

        def write2(it, m, tag):
            bs_ref[it:it + 1, :] = m
            bid_ref[it:it + 1, :] = tag

        _extract_top(cand, tags, K, write2)
        bs = bs_ref[...]
        e = jnp.exp(bs - jnp.max(bs, axis=0, keepdims=True))
        gt_ref[h * K:(h + 1) * K, :] = e / jnp.sum(e, axis=0, keepdims=True)
        bid = bid_ref[...]
        at_ref[h * K:(h + 1) * K, :] = (bid // PEER_NKEYS) % PEER_NKEYS
        bt_ref[h * K:(h + 1) * K, :] = bid % PEER_NKEYS
    for blk in range(tm // LANES):
        sl = slice(blk * LANES, (blk + 1) * LANES)
        a_ref[sl, :] = at_ref[:, sl].T
        b_ref[sl, :] = bt_ref[:, sl].T
        g_ref[sl, :] = gt_ref[:, sl].T


def peer_topk(q_bf16, sub_keys, tm=256):
    n = q_bf16.shape[0]
    groups, flat = _peer_cand_layout()
    nc = flat.shape[0]
    flat_b = jnp.asarray(np.broadcast_to(flat[:, None], (nc, tm)))
    hk = PEER_HEADS * PEER_TOPK
    row = lambda dt: jax.ShapeDtypeStruct((n, hk), dt)
    return pl.pallas_call(
        functools.partial(_peer_topk_kernel, tm=tm, groups=groups),
        out_shape=[row(jnp.int32), row(jnp.int32), row(F32)],
        grid=(n // tm,),
        in_specs=[pl.BlockSpec((tm, q_bf16.shape[1]), lambda i: (i, 0)),
                  pl.BlockSpec(sub_keys.shape, lambda i: (0, 0, 0, 0)),
                  pl.BlockSpec((nc, tm), lambda i: (0, 0))],
        out_specs=[pl.BlockSpec((tm, hk), lambda i: (i, 0))] * 3,
        scratch_shapes=[pltpu.VMEM((PEER_TOPK, tm), F32), pltpu.VMEM((PEER_TOPK, tm), F32),
                        pltpu.VMEM((PEER_TOPK, tm), jnp.int32), pltpu.VMEM((PEER_TOPK, tm), jnp.int32),
                        pltpu.VMEM((PEER_TOPK, tm), F32), pltpu.VMEM((PEER_TOPK, tm), jnp.int32),
                        pltpu.VMEM((hk, tm), jnp.int32), pltpu.VMEM((hk, tm), jnp.int32),
                        pltpu.VMEM((hk, tm), F32)],
        compiler_params=_cparams("parallel"),
        name="peer_topk",
    )(q_bf16, sub_keys.astype(BF16), flat_b)


_W_ROWS = PEER_NKEYS
_W_PITCH = _W_ROWS + SUBLANES


def _peer_apply_kernel(x_ref, ut_ref, v_ref, a_ref, b_ref, g_ref, r_ref, o_ref, w_ref, *, tm, ec, unroll):
    c = pl.program_id(1)
    slabs = ec // PEER_NKEYS
    steps = _W_ROWS // slabs
    half = c // steps
    cc = c % steps

    @pl.when(c == 0)
    def _():
        o_ref[...] = r_ref[...]

    @pl.when(cc == 0)
    def _():
        sub_a = half * _W_ROWS + lax.broadcasted_iota(jnp.int32, (_W_ROWS, LANES), 0)
        sub_b = lax.broadcasted_iota(jnp.int32, (PEER_NKEYS, LANES), 0)

        def build(tb, carry):
            for tt in range(unroll):
                t = tb * unroll + tt
                a = a_ref[pl.ds(t, 1), :]
                b = b_ref[pl.ds(t, 1), :]
                g = g_ref[pl.ds(t, 1), :]
                pt = jnp.where(sub_a == a, g, 0.0).astype(BF16)
                qt = jnp.where(sub_b == b, 1.0, 0.0).astype(BF16)
                w_ref[pl.ds(pl.multiple_of(t * _W_PITCH, SUBLANES), _W_ROWS), :] = _dot_nt(pt, qt)
            return carry

        lax.fori_loop(0, tm // unroll, build, 0)

    act = _dot(x_ref[...], ut_ref[...])
    w = jnp.concatenate([w_ref[pl.ds(cc * slabs + s, tm, stride=_W_PITCH), :] for s in range(slabs)], axis=1)
    o_ref[...] += _dot((_gelu(act) * w).astype(BF16), v_ref[...])


def peer_apply(xn_bf16, ut_bf16, v_bf16, a_idx, b_idx, gw, resid, tm=256, ec=2048, unroll=64):
    n, d = xn_bf16.shape
    ne = v_bf16.shape[0]
    assert ne == PEER_NKEYS * PEER_NKEYS and _W_ROWS % (ec // PEER_NKEYS) == 0
    tok = lambda w: pl.BlockSpec((tm, w), lambda i, c: (i, 0))
    return pl.pallas_call(
        functools.partial(_peer_apply_kernel, tm=tm, ec=ec, unroll=unroll),
        out_shape=jax.ShapeDtypeStruct((n, d), F32),
        grid=(n // tm, ne // ec),
        in_specs=[tok(d),
                  pl.BlockSpec((d, ec), lambda i, c: (0, c)),
                  pl.BlockSpec((ec, d), lambda i, c: (c, 0)),
                  tok(a_idx.shape[1]), tok(b_idx.shape[1]), tok(gw.shape[1]), tok(d)],
        out_specs=tok(d),
        scratch_shapes=[pltpu.VMEM((tm * _W_PITCH, LANES), F32)],
        compiler_params=pltpu.CompilerParams(dimension_semantics=("parallel", "arbitrary"),
                                             vmem_limit_bytes=56 * 1024 * 1024),
        name="peer_apply",
    )(xn_bf16, ut_bf16, v_bf16, a_idx, b_idx, gw, resid)


def _odd_mixer_kernel(h_ref, g_ref, win_ref, cw_ref, cb_ref, wra_ref, bra_ref, wri_ref, bri_ref, lam_ref, wout_ref,
                      o_ref, xbuf_ref, carry_ref, *, tc):
    j = pl.program_id(1)
    d_rnn = cw_ref.shape[1]
    pad = SUBLANES

    @pl.when(j == 0)
    def _():
        xbuf_ref[0:pad, :] = jnp.zeros((pad, d_rnn), F32)
        carry_ref[...] = jnp.zeros(carry_ref.shape, F32)

    h_in = h_ref[...]
    z = _dot(_rmsnorm_rows(h_in, g_ref[...]).astype(BF16), win_ref[...])
    gate = z[:, :d_rnn]
    xbuf_ref[pad:pad + tc, :] = z[:, d_rnn:]
    xc = cb_ref[...] + cw_ref[0:1, :] * xbuf_ref[pl.ds(pad - 3, tc), :]
    for w in range(1, CONV_W):
        xc = xc + cw_ref[w:w + 1, :] * xbuf_ref[pl.ds(pad - 3 + w, tc), :]
    xbuf_ref[0:pad, :] = xbuf_ref[tc:tc + pad, :]

    xcb = xc.astype(BF16)
    nb = d_rnn // RNN_BLOCK_W
    blk = lambda w_ref: jnp.concatenate(
        [_dot(xcb[:, n * RNN_BLOCK_W:(n + 1) * RNN_BLOCK_W], w_ref[n]) for n in range(nb)], axis=1)
    r = _sigmoid(blk(wra_ref) + bra_ref[...])
    gi = _sigmoid(blk(wri_ref) + bri_ref[...])
    nl = -lam_ref[...]
    softplus = jnp.maximum(nl, 0.0) + jnp.log(1.0 + jnp.exp(-jnp.abs(nl)))
    log_a = (-LRU_C) * r * softplus
    a = jnp.exp(log_a)
    u = jnp.sqrt(1.0 - jnp.exp(2.0 * log_a)) * (gi * xc)

    row = lax.broadcasted_iota(jnp.int32, (tc, d_rnn), 0)
    shift = 1
    while shift < tc:
        valid = row >= shift
        a_prev = jnp.where(valid, pltpu.roll(a, shift, 0), 1.0)
        u_prev = jnp.where(valid, pltpu.roll(u, shift, 0), 0.0)
        u = u + a * u_prev
        a = a * a_prev
        shift *= 2
    hs = u + a * carry_ref[...]
    carry_ref[...] = hs[tc - 1:tc, :]

    y = (_gelu(gate) * hs).astype(BF16)
    o_ref[...] = h_in + _dot(y, wout_ref[...])


def odd_mixer_residual(h, norm_g, w_in, conv_w, conv_b, w_ra, b_ra, w_ri, b_ri, lam, w_out, tc=256):
    b, s, d = h.shape
    d_rnn = conv_w.shape[1]
    full = lambda a: pl.BlockSpec(a.shape, lambda bb, j, nd=a.ndim: (0,) * nd)
    row = lambda v: v.astype(F32).reshape(1, -1)
    args = [norm_g.astype(F32).reshape(1, d), w_in.astype(BF16), conv_w.astype(F32), row(conv_b),
            w_ra.astype(BF16), row(b_ra), w_ri.astype(BF16), row(b_ri), row(lam), w_out.astype(BF16)]
    return pl.pallas_call(
        functools.partial(_odd_mixer_kernel, tc=tc),
        out_shape=jax.ShapeDtypeStruct((b, s, d), F32),
        grid=(b, s // tc),
        in_specs=[pl.BlockSpec((None, tc, d), lambda bb, j: (bb, j, 0))] + [full(a) for a in args],
        out_specs=pl.BlockSpec((None, tc, d), lambda bb, j: (bb, j, 0)),
        scratch_shapes=[pltpu.VMEM((tc + 2 * SUBLANES, d_rnn), F32), pltpu.VMEM((1, d_rnn), F32)],
        compiler_params=_cparams("parallel", "arbitrary"),
        name="odd_mixer",
    )(h, *args)


def _rmsnorm_kernel(x_ref, g_ref, o_ref):
    o_ref[...] = _rmsnorm_rows(x_ref[...], g_ref[...])


def rmsnorm_rows(x2d, g, tm=512):
    n, d = x2d.shape
    return pl.pallas_call(
        _rmsnorm_kernel,
        out_shape=jax.ShapeDtypeStruct((n, d), F32),
        grid=(n // tm,),
        in_specs=[pl.BlockSpec((tm, d), lambda i: (i, 0)), pl.BlockSpec((1, d), lambda i: (0, 0))],
        out_specs=pl.BlockSpec((tm, d), lambda i: (i, 0)),
        compiler_params=_cparams("parallel"),
        name="final_rmsnorm",
    )(x2d, g.astype(F32).reshape(1, d))


def even_mixer_residual(h, norm_g, w_in, b_f, cmp_pos, w_cmp_k, w_cmp_v, w_out):
    b, s, d = h.shape
    h2d = h.reshape(b * s, d)
    w_ext, wt_ext = _even_w_ext(w_in)
    outs = norm_proj(h2d, norm_g.astype(F32), w_ext, _EVEN_SEGS, _EVEN_DTYPES, wt_ext, _EVEN_TSEGS, name="even_in_proj")
    fq, fk, nq, ncmp, nk, small = [o.reshape(b, s, -1) for o in outs[:6]]
    fvt, svt, wvt = outs[6:]
    c = fox_gate_cumsum(small[..., :N_FOX_HEADS].transpose(0, 2, 1), b_f)
    o_fox = fox_attention(fq, fk, fvt, c)
    kc, vct = nsa_compress(ncmp, cmp_pos, w_cmp_k, w_cmp_v)
    gates_t = small[..., N_FOX_HEADS:N_FOX_HEADS + NSA_GATES].reshape(b, s, N_NSA_KV, 3 * NSA_GROUP).transpose(0, 2, 3, 1)
    gates_t = jnp.pad(gates_t, ((0, 0), (0, 0), (0, 2 * SUBLANES - 3 * NSA_GROUP), (0, 0)))
    o_nsa = nsa_attention(nq, kc, vct, nk, svt, wvt, gates_t)
    nf = N_FOX_HEADS * HEAD_DIM
    w_n = w_out[nf:].reshape(N_NSA_HEADS, HEAD_DIM, d)
    half_n = (np.arange(N_NSA_HEADS) // NSA_GROUP)[:, None, None, None] == np.arange(2)[None, :, None, None]
    w_n = jnp.where(half_n, w_n[:, None], 0.0).reshape(N_NSA_HEADS * LANES, d)
    pairs = [(o_fox.reshape(b * s, -1), w_out[:nf].astype(BF16)), (o_nsa.reshape(b * s, -1), w_n.astype(BF16))]
    return proj_residual(pairs, h2d, name="even_out_proj").reshape(b, s, d)


def peer_ffn_residual(h2d, norm_g, w_q, sub_keys, u_tab, v_tab):
    d = h2d.shape[1]
    nq = w_q.shape[1]
    q, xn = norm_proj(h2d, norm_g, w_q.astype(BF16), ((0, nq, 1.0), (None, d, 1.0)), (BF16, BF16), name="peer_query")
    a_idx, b_idx, gw = peer_topk(q, sub_keys)
    return peer_apply(xn, u_tab.astype(BF16).T, v_tab.astype(BF16), a_idx, b_idx, gw, h2d)


def kernel(x, even_norm_g, even_w_in, even_b_f, even_cmp_pos, even_w_cmp_k, even_w_cmp_v, even_w_out, odd_norm_g, odd_w_in, odd_conv_w, odd_conv_b, odd_w_ra, odd_b_ra, odd_w_ri, odd_b_ri, odd_lam, odd_w_out, ffn_norm_g, peer_w_q, peer_sub_keys, peer_u, peer_v, final_g):
    b, s, d = x.shape
    depth = ffn_norm_g.shape[0]
    h = x
    for layer in range(depth):
        j = layer // 2
        if layer % 2 == 0:
            h = even_mixer_residual(h, even_norm_g[j], even_w_in[j], even_b_f[j], even_cmp_pos[j],
                                    even_w_cmp_k[j], even_w_cmp_v[j], even_w_out[j])
        else:
            h = odd_mixer_residual(h, odd_norm_g[j], odd_w_in[j], odd_conv_w[j], odd_conv_b[j], odd_w_ra[j],
                                   odd_b_ra[j], odd_w_ri[j], odd_b_ri[j], odd_lam[j], odd_w_out[j])
        h = peer_ffn_residual(h.reshape(b * s, d), ffn_norm_g[layer].astype(F32), peer_w_q[layer],
                              peer_sub_keys[layer], peer_u[layer], peer_v[layer]).reshape(b, s, d)
    return rmsnorm_rows(h.reshape(b * s, d), final_g).reshape(b, s, d)
```
